```python
import functools
import math
import jax
import jax.numpy as jnp
from jax import lax
import numpy as np

D_MODEL = 4096
BATCH = 4
SEQ = 2048
DEPTH = 2
DEC_BATCH = 8
DEC_SEQ = 1
PAST_LEN = 16384
PAGE_SIZE = 128

ATTN_HEADS = 16
HEAD_DIM = 128
ATTN_WIDTH = ATTN_HEADS * HEAD_DIM
ATTN_SCALE = HEAD_DIM ** -0.5
Q_BLOCK = 128
FORGET_BIAS_LO = 4.0
FORGET_BIAS_HI = 10.0
CONV_WIDTH = D_MODEL // 2
CONV_K = 3
SSM_WIDTH = D_MODEL // 2
SSM_GROUP = 16
SSM_GROUPS = SSM_WIDTH // SSM_GROUP
SSM_STATE = 64
N_BRANCH = 3
IN_SIZES = (ATTN_WIDTH, ATTN_WIDTH, ATTN_WIDTH, ATTN_HEADS, CONV_WIDTH, CONV_WIDTH, CONV_WIDTH, SSM_WIDTH, N_BRANCH * D_MODEL)
N_IN = sum(IN_SIZES)
MOE_GROUPS = 4
EXPERTS_PER_GROUP = 8
N_EXPERTS = MOE_GROUPS * EXPERTS_PER_GROUP
EXPERT_HIDDEN = D_MODEL // 8
TOP_K = 2
MOE_BLOCK = 128
ALPHA = (2 * DEPTH) ** 0.25
BETA = (8 * DEPTH) ** -0.25
LN_EPS = 1e-5

kernel_name = 'hybrid_fox_s5_conv_hmoe_decode_step'


def layer_norm(x, g, b):
    xf = x.astype(jnp.float32)
    mu = jnp.mean(xf, axis=-1, keepdims=True)
    var = jnp.mean(jnp.square(xf - mu), axis=-1, keepdims=True)
    y = (xf - mu) * lax.rsqrt(var + LN_EPS) * g.astype(jnp.float32) + b.astype(jnp.float32)
    return y.astype(x.dtype)


def ada_modulation(c, w_ada, b_ada):
    m = jax.nn.silu(c) @ w_ada + b_ada
    return jnp.split(m[:, None, :], 6, axis=-1)


def split_columns(z):
    parts, start = [], 0
    for size in IN_SIZES:
        parts.append(z[..., start:start + size])
        start += size
    return parts


def fox_attention_prompt(q, k, v, logf):
    b, t = q.shape[:2]
    n_blk = t // Q_BLOCK
    f_cum = jnp.cumsum(logf.astype(jnp.float32), axis=1).transpose(0, 2, 1)
    q_blk = q.reshape(b, n_blk, Q_BLOCK, ATTN_HEADS, HEAD_DIM).swapaxes(0, 1)
    k_pos = jnp.arange(t)

    def one_block(args):
        q_i, i = args
        q_pos = i * Q_BLOCK + jnp.arange(Q_BLOCK)
        f_q = lax.dynamic_slice_in_dim(f_cum, i * Q_BLOCK, Q_BLOCK, axis=2)
        s = jnp.einsum('bqhd,bkhd->bhqk', q_i, k).astype(jnp.float32) * ATTN_SCALE
        s = s + f_q[..., :, None] - f_cum[..., None, :]
        s = jnp.where(k_pos[None, :] <= q_pos[:, None], s, -jnp.inf)
        p = jax.nn.softmax(s, axis=-1).astype(v.dtype)
        return jnp.einsum('bhqk,bkhd->bqhd', p, v)

    out = lax.map(one_block, (q_blk, jnp.arange(n_blk)))
    return out.swapaxes(0, 1).reshape(b, t, ATTN_HEADS, HEAD_DIM)


def fox_attention_sample(q, k, v, logf, k_past, v_past, logf_past):
    n = q.shape[1]
    past = k_past.shape[1]
    lp = logf_past.astype(jnp.float32)
    f_past = lp - lax.cumsum(lp, axis=1, reverse=True)
    f_new = jnp.cumsum(logf.astype(jnp.float32), axis=1)
    f_keys = jnp.concatenate([f_past, f_new], axis=1).transpose(0, 2, 1)
    f_q = f_new.transpose(0, 2, 1)
    k_all = jnp.concatenate([k_past.astype(k.dtype), k], axis=1)
    v_all = jnp.concatenate([v_past.astype(v.dtype), v], axis=1)
    s = jnp.einsum('bqhd,bkhd->bhqk', q, k_all).astype(jnp.float32) * ATTN_SCALE
    s = s + f_q[..., :, None] - f_keys[..., None, :]
    k_pos = jnp.arange(past + n)
    q_pos = past + jnp.arange(n)
    s = jnp.where(k_pos[None, :] <= q_pos[:, None], s, -jnp.inf)
    p = jax.nn.softmax(s, axis=-1).astype(v_all.dtype)
    return jnp.einsum('bhqk,bkhd->bqhd', p, v_all)


def short_conv(v, w, prev):
    t = v.shape[1]
    vp = jnp.concatenate([prev.astype(v.dtype), v], axis=1)
    y = sum(w[j] * vp[:, j:j + t] for j in range(CONV_K))
    return y, vp[:, -(CONV_K - 1):]


def ssm_discretize(lam_re, lam_im, log_dt, b_re, b_im):
    lam_re = lam_re.astype(jnp.float32)
    lam_im = lam_im.astype(jnp.float32)
    dt = jnp.exp(log_dt.astype(jnp.float32))[:, None]
    mag = jnp.exp(lam_re * dt)
    ang = lam_im * dt
    a_re, a_im = mag * jnp.cos(ang), mag * jnp.sin(ang)
    den = lam_re * lam_re + lam_im * lam_im
    n_re, n_im = a_re - 1.0, a_im
    f_re = (n_re * lam_re + n_im * lam_im) / den
    f_im = (n_im * lam_re - n_re * lam_im) / den
    b_re = b_re.astype(jnp.float32)
    b_im = b_im.astype(jnp.float32)
    bb_re = f_re[..., None] * b_re - f_im[..., None] * b_im
    bb_im = f_re[..., None] * b_im + f_im[..., None] * b_re
    return a_re, a_im, bb_re, bb_im


def _ssm_combine(left, right):
    la_r, la_i, lx_r, lx_i = left
    ra_r, ra_i, rx_r, rx_i = right
    return (la_r * ra_r - la_i * ra_i, la_r * ra_i + la_i * ra_r,
            ra_r * lx_r - ra_i * lx_i + rx_r, ra_r * lx_i + ra_i * lx_r + rx_i)


def ssm_branch(u, lp, h0_re, h0_im):
    b, t, _ = u.shape
    a_re, a_im, bb_re, bb_im = ssm_discretize(lp['ssm_lambda_re'], lp['ssm_lambda_im'], lp['ssm_log_dt'],
                                              lp['ssm_b_re'], lp['ssm_b_im'])
    uf = u.astype(jnp.float32)
    ug = uf.reshape(b, t, SSM_GROUPS, SSM_GROUP)
    x_re = jnp.einsum('btgc,gpc->btgp', ug, bb_re)
    x_im = jnp.einsum('btgc,gpc->btgp', ug, bb_im)
    pw_re, pw_im, h_re, h_im = lax.associative_scan(
        _ssm_combine, (jnp.broadcast_to(a_re, x_re.shape), jnp.broadcast_to(a_im, x_im.shape), x_re, x_im), axis=1)
    h0_re = h0_re.astype(jnp.float32)[:, None]
    h0_im = h0_im.astype(jnp.float32)[:, None]
    h_re = h_re + pw_re * h0_re - pw_im * h0_im
    h_im = h_im + pw_re * h0_im + pw_im * h0_re
    y = (jnp.einsum('btgp,gcp->btgc', h_re, lp['ssm_c_re'].astype(jnp.float32))
         - jnp.einsum('btgp,gcp->btgc', h_im, lp['ssm_c_im'].astype(jnp.float32)))
    y = y.reshape(b, t, SSM_WIDTH) + lp['ssm_d'].astype(jnp.float32) * uf
    z = jax.nn.gelu(y).astype(u.dtype)
    out = z * jax.nn.sigmoid(z @ lp['ssm_w_glu'])
    return out, h_re[:, -1], h_im[:, -1]


def routed_experts(tok, e_idx, wts, w_gate, w_up, w_down):
    n, d = tok.shape
    n_assign = n * TOP_K
    n_blocks = -(-(n_assign + N_EXPERTS * (MOE_BLOCK - 1)) // MOE_BLOCK)
    flat_e = e_idx.reshape(-1)
    order = jnp.argsort(flat_e)
    e_sorted = flat_e[order]
    counts = jnp.bincount(flat_e, length=N_EXPERTS)
    padded = (counts + MOE_BLOCK - 1) // MOE_BLOCK * MOE_BLOCK
    pad_end = jnp.cumsum(padded)
    pad_start = pad_end - padded
    start = jnp.cumsum(counts) - counts
    dest = (pad_start[e_sorted] + jnp.arange(n_assign) - start[e_sorted]).astype(jnp.int32)
    src_tok = jnp.full((n_blocks * MOE_BLOCK,), n, jnp.int32).at[dest].set((order // TOP_K).astype(jnp.int32))
    tok_pad = jnp.concatenate([tok, jnp.zeros((1, d), tok.dtype)], axis=0)
    x_blk = tok_pad[src_tok].reshape(n_blocks, MOE_BLOCK, d)
    blk_e = jnp.minimum(jnp.searchsorted(pad_end, jnp.arange(n_blocks) * MOE_BLOCK, side='right'), N_EXPERTS - 1)

    def expert_block(args):
        xb, e = args
        return (jax.nn.silu(xb @ w_gate[e]) * (xb @ w_up[e])) @ w_down[e]

    y_rows = lax.map(expert_block, (x_blk, blk_e)).reshape(n_blocks * MOE_BLOCK, d)
    row_of_assign = jnp.zeros((n_assign,), jnp.int32).at[order].set(dest)
    y_assign = y_rows[row_of_assign].reshape(n, TOP_K, d)
    return jnp.einsum('nk,nkd->nd', wts.astype(y_assign.dtype), y_assign)


def hierarchical_moe(h, lp):
    b, t, d = h.shape
    tok = h.reshape(b * t, d)
    n = tok.shape[0]
    rows = jnp.arange(n)
    g_logit = (tok @ lp['router_w_group']).astype(jnp.float32) + lp['router_b_group'].astype(jnp.float32)
    g_prob = jax.nn.softmax(g_logit, axis=-1)
    g_idx = jnp.argmax(g_logit, axis=-1).astype(jnp.int32)
    e_logit = ((tok @ lp['router_w_expert']).astype(jnp.float32) + lp['router_b_expert'].astype(jnp.float32))
    e_logit = e_logit.reshape(n, MOE_GROUPS, EXPERTS_PER_GROUP)[rows, g_idx]
    top_logit, top_local = lax.top_k(e_logit, TOP_K)
    wts = jax.nn.softmax(top_logit, axis=-1) * g_prob[rows, g_idx][:, None]
    e_idx = g_idx[:, None] * EXPERTS_PER_GROUP + top_local.astype(jnp.int32)
    y = routed_experts(tok, e_idx, wts, lp['moe_w_gate'], lp['moe_w_up'], lp['moe_w_down'])
    return y.reshape(b, t, d)


def mixer_sublayer(u, lp, attn_fn, conv_prev, ssm_prev_re, ssm_prev_im):
    b, t, _ = u.shape
    q, k, v, f_logit, conv_b, conv_c, conv_x, ssm_u, gate_logit = split_columns(u @ lp['w_in'])
    heads = (b, t, ATTN_HEADS, HEAD_DIM)
    q, k, v = q.reshape(heads), k.reshape(heads), v.reshape(heads)
    logf = jax.nn.log_sigmoid((f_logit + lp['b_forget']).astype(jnp.float32))
    y_attn = attn_fn(q, k, v, logf).reshape(b, t, ATTN_WIDTH)
    conv_y, conv_new = short_conv(conv_c * conv_x, lp['conv_w'], conv_prev)
    y_conv = conv_b * conv_y
    y_ssm, ssm_re_new, ssm_im_new = ssm_branch(ssm_u, lp, ssm_prev_re, ssm_prev_im)
    gates = jax.nn.sigmoid(gate_logit).reshape(b, t, N_BRANCH, D_MODEL)
    merged = (gates[:, :, 0] * (y_attn @ lp['w_br_attn'])
              + gates[:, :, 1] * (y_conv @ lp['w_br_conv'])
              + gates[:, :, 2] * (y_ssm @ lp['w_br_ssm']))
    return merged @ lp['w_o'], (k, v, logf), conv_new, ssm_re_new, ssm_im_new


def trunk_layer(x, c, lp, attn_fn, conv_prev, ssm_prev_re, ssm_prev_im):
    sh1, sc1, g1, sh2, sc2, g2 = ada_modulation(c, lp['w_ada'], lp['b_ada'])
    mix, kv_rows, conv_new, ssm_re_new, ssm_im_new = mixer_sublayer(
        x * (1 + sc1) + sh1, lp, attn_fn, conv_prev, ssm_prev_re, ssm_prev_im)
    x = layer_norm(ALPHA * x + (1 + g1) * mix, lp['ln1_g'], lp['ln1_b'])
    ffn = hierarchical_moe(x * (1 + sc2) + sh2, lp)
    x = layer_norm(ALPHA * x + (1 + g2) * ffn, lp['ln2_g'], lp['ln2_b'])
    return x, kv_rows, conv_new, ssm_re_new, ssm_im_new


def paged_rows(pool, layer, page_table):
    rows = pool[layer][page_table]
    return rows.reshape(page_table.shape[0], -1, *rows.shape[3:])


def setup_inputs(seed: int = 0) -> dict:
    key = jax.random.key(seed)
    ks = iter(jax.random.split(key, 64))
    f32 = jnp.float32

    def nrm(shape, scale=1.0):
        return jax.random.normal(next(ks), shape, f32) * scale

    d = D_MODEL
    n_pages = PAST_LEN // PAGE_SIZE
    n_used = DEC_BATCH * n_pages
    n_pool = n_used + max(1, n_used // 4)
    page_table = jax.random.permutation(next(ks), n_pool)[:n_used].reshape(DEC_BATCH, n_pages).astype(jnp.int32)
    lam_im = jnp.pi * jnp.arange(SSM_STATE, dtype=f32)
    head_bias = jnp.linspace(FORGET_BIAS_LO, FORGET_BIAS_HI, ATTN_HEADS, dtype=f32)
    return {
        'x_prompt': nrm((BATCH, SEQ, d)),
        'x_sample': nrm((DEC_BATCH, DEC_SEQ, d)),
        'c_prompt': nrm((BATCH, d)),
        'c_sample': nrm((DEC_BATCH, d)),
        'cache_k': nrm((DEPTH, n_pool, PAGE_SIZE, ATTN_HEADS, HEAD_DIM)),
        'cache_v': nrm((DEPTH, n_pool, PAGE_SIZE, ATTN_HEADS, HEAD_DIM)),
        'cache_logf': jax.nn.log_sigmoid(head_bias + nrm((DEPTH, n_pool, PAGE_SIZE, ATTN_HEADS), 0.5)),
        'page_table': page_table,
        'state_conv': nrm((DEPTH, DEC_BATCH, CONV_K - 1, CONV_WIDTH)),
        'state_ssm_re': nrm((DEPTH, DEC_BATCH, SSM_GROUPS, SSM_STATE), 0.3),
        'state_ssm_im': nrm((DEPTH, DEC_BATCH, SSM_GROUPS, SSM_STATE), 0.3),
        'w_ada': nrm((DEPTH, d, 6 * d), 0.1 * d ** -0.5),
        'b_ada': nrm((DEPTH, 6 * d), 0.01),
        'ln1_g': 1.0 + nrm((DEPTH, d), 0.01),
        'ln1_b': nrm((DEPTH, d), 0.01),
        'ln2_g': 1.0 + nrm((DEPTH, d), 0.01),
        'ln2_b': nrm((DEPTH, d), 0.01),
        'w_in': nrm((DEPTH, d, N_IN), d ** -0.5),
        'b_forget': head_bias + nrm((DEPTH, ATTN_HEADS), 0.1),
        'conv_w': nrm((DEPTH, CONV_K, CONV_WIDTH), CONV_K ** -0.5),
        'ssm_lambda_re': -0.5 + nrm((DEPTH, SSM_GROUPS, SSM_STATE), 0.01),
        'ssm_lambda_im': lam_im + nrm((DEPTH, SSM_GROUPS, SSM_STATE), 0.01),
        'ssm_log_dt': jax.random.uniform(next(ks), (DEPTH, SSM_GROUPS), f32, math.log(1e-3), math.log(1e-1)),
        'ssm_b_re': nrm((DEPTH, SSM_GROUPS, SSM_STATE, SSM_GROUP), (2 * SSM_GROUP) ** -0.5),
        'ssm_b_im': nrm((DEPTH, SSM_GROUPS, SSM_STATE, SSM_GROUP), (2 * SSM_GROUP) ** -0.5),
        'ssm_c_re': nrm((DEPTH, SSM_GROUPS, SSM_GROUP, SSM_STATE), (2 * SSM_STATE) ** -0.5),
        'ssm_c_im': nrm((DEPTH, SSM_GROUPS, SSM_GROUP, SSM_STATE), (2 * SSM_STATE) ** -0.5),
        'ssm_d': nrm((DEPTH, SSM_WIDTH)),
        'ssm_w_glu': nrm((DEPTH, SSM_WIDTH, SSM_WIDTH), SSM_WIDTH ** -0.5),
        'w_br_attn': nrm((DEPTH, ATTN_WIDTH, d), ATTN_WIDTH ** -0.5),
        'w_br_conv': nrm((DEPTH, CONV_WIDTH, d), CONV_WIDTH ** -0.5),
        'w_br_ssm': nrm((DEPTH, SSM_WIDTH, d), SSM_WIDTH ** -0.5),
        'w_o': nrm((DEPTH, d, d), BETA * d ** -0.5),
        'router_w_group': nrm((DEPTH, d, MOE_GROUPS), d ** -0.5),
        'router_b_group': nrm((DEPTH, MOE_GROUPS), 0.01),
        'router_w_expert': nrm((DEPTH, d, N_EXPERTS), d ** -0.5),
        'router_b_expert': nrm((DEPTH, N_EXPERTS), 0.01),
        'moe_w_gate': nrm((DEPTH, N_EXPERTS, d, EXPERT_HIDDEN), d ** -0.5),
        'moe_w_up': nrm((DEPTH, N_EXPERTS, d, EXPERT_HIDDEN), d ** -0.5),
        'moe_w_down': nrm((DEPTH, N_EXPERTS, EXPERT_HIDDEN, d), BETA * EXPERT_HIDDEN ** -0.5),
    }


def reference(x_prompt, x_sample, c_prompt, c_sample, cache_k, cache_v, cache_logf, page_table,
              state_conv, state_ssm_re, state_ssm_im,
              w_ada, b_ada, ln1_g, ln1_b, ln2_g, ln2_b, w_in, b_forget, conv_w,
              ssm_lambda_re, ssm_lambda_im, ssm_log_dt, ssm_b_re, ssm_b_im, ssm_c_re, ssm_c_im, ssm_d, ssm_w_glu,
              w_br_attn, w_br_conv, w_br_ssm, w_o,
              router_w_group, router_b_group, router_w_expert, router_b_expert,
              moe_w_gate, moe_w_up, moe_w_down):
    stacked = dict(w_ada=w_ada, b_ada=b_ada, ln1_g=ln1_g, ln1_b=ln1_b, ln2_g=ln2_g, ln2_b=ln2_b,
                   w_in=w_in, b_forget=b_forget, conv_w=conv_w,
                   ssm_lambda_re=ssm_lambda_re, ssm_lambda_im=ssm_lambda_im, ssm_log_dt=ssm_log_dt,
                   ssm_b_re=ssm_b_re, ssm_b_im=ssm_b_im, ssm_c_re=ssm_c_re, ssm_c_im=ssm_c_im,
                   ssm_d=ssm_d, ssm_w_glu=ssm_w_glu,
                   w_br_attn=w_br_attn, w_br_conv=w_br_conv, w_br_ssm=w_br_ssm, w_o=w_o,
                   router_w_group=router_w_group, router_b_group=router_b_group,
                   router_w_expert=router_w_expert, router_b_expert=router_b_expert,
                   moe_w_gate=moe_w_gate, moe_w_up=moe_w_up, moe_w_down=moe_w_down)
    bp = x_prompt.shape[0]
    conv0 = jnp.zeros((bp, CONV_K - 1, CONV_WIDTH), x_prompt.dtype)
    ssm0 = jnp.zeros((bp, SSM_GROUPS, SSM_STATE), jnp.float32)
    xp, xs = x_prompt, x_sample
    kp, vp, fp, cp, srp, sip = [], [], [], [], [], []
    ks_, vs_, fs_, cs_, srs, sis = [], [], [], [], [], []
    for l in range(DEPTH):
        lp = {name: arr[l] for name, arr in stacked.items()}
        xp, (k_p, v_p, f_p), conv_p, sre_p, sim_p = trunk_layer(
            xp, c_prompt, lp, fox_attention_prompt, conv0, ssm0, ssm0)
        attn_s = functools.partial(fox_attention_sample,
                                   k_past=paged_rows(cache_k, l, page_table),
                                   v_past=paged_rows(cache_v, l, page_table),
                                   logf_past=paged_rows(cache_logf, l, page_table))
        xs, (k_s, v_s, f_s), conv_s, sre_s, sim_s = trunk_layer(
            xs, c_sample, lp, attn_s, state_conv[l], state_ssm_re[l], state_ssm_im[l])
        kp.append(k_p); vp.append(v_p); fp.append(f_p); cp.append(conv_p); srp.append(sre_p); sip.append(sim_p)
        ks_.append(k_s); vs_.append(v_s); fs_.append(f_s); cs_.append(conv_s); srs.append(sre_s); sis.append(sim_s)
    return (xp, xs,
            jnp.stack(kp), jnp.stack(vp), jnp.stack(fp),
            jnp.stack(ks_), jnp.stack(vs_), jnp.stack(fs_),
            jnp.stack(cp), jnp.stack(cs_),
            jnp.stack(srp), jnp.stack(sip), jnp.stack(srs), jnp.stack(sis))
```

```python
import functools
import math

import jax
import jax.numpy as jnp
from jax import lax
from jax.experimental import pallas as pl
from jax.experimental.pallas import tpu as pltpu

F32 = jnp.float32
BF16 = jnp.bfloat16
I32 = jnp.int32

LANES = 128
VMEM_LIMIT = 56 * 1024 * 1024
LN_EPS = 1e-5
SSM_CHUNK = 8
MOE_ROWS = 128
NT_DIMS = (((1,), (1,)), ((), ()))


def _cparams(*sem):
    return pltpu.CompilerParams(dimension_semantics=sem, vmem_limit_bytes=VMEM_LIMIT)


def _pick(n, pref):
    if n <= pref:
        return n
    b = pref
    while n % b:
        b //= 2
    return b


def _ada_kernel(c_ref, w_ref, b_ref, o_ref):
    c = c_ref[...]
    s = (c * jax.nn.sigmoid(c)).astype(BF16)
    o_ref[...] = jnp.dot(s, w_ref[...].astype(BF16), preferred_element_type=F32) + b_ref[...]


def ada_modulation(c_all, w_ada, b_ada):
    depth, d, n6 = w_ada.shape
    r = c_all.shape[0]
    bn = _pick(n6, 512)
    return pl.pallas_call(
        _ada_kernel,
        grid=(depth, n6 // bn),
        in_specs=[
            pl.BlockSpec((r, d), lambda l, j: (0, 0)),
            pl.BlockSpec((None, d, bn), lambda l, j: (l, 0, j)),
            pl.BlockSpec((None, 1, bn), lambda l, j: (l, 0, j)),
        ],
        out_specs=pl.BlockSpec((None, r, bn), lambda l, j: (l, 0, j)),
        out_shape=jax.ShapeDtypeStruct((depth, r, n6), F32),
        compiler_params=_cparams("arbitrary", "arbitrary"),
        name="ada_modulation",
    )(c_all, w_ada, b_ada.reshape(depth, 1, n6))


def _mod_spec(mod, bt):
    if mod.shape[1] == 1:
        return pl.BlockSpec((1, 1, mod.shape[2]), lambda b, t: (b, 0, 0))
    return pl.BlockSpec((1, bt, mod.shape[2]), lambda b, t: (b, t, 0))


def _modulate_kernel(x_ref, sc_ref, sh_ref, o_ref):
    o_ref[...] = (x_ref[...] * (1.0 + sc_ref[...]) + sh_ref[...]).astype(o_ref.dtype)


def modulate(x3, sc, sh):
    nb, t, d = x3.shape
    bt = _pick(t, 512)
    return pl.pallas_call(
        _modulate_kernel,
        grid=(nb, t // bt),
        in_specs=[pl.BlockSpec((1, bt, d), lambda b, i: (b, i, 0)), _mod_spec(sc, bt), _mod_spec(sh, bt)],
        out_specs=pl.BlockSpec((1, bt, d), lambda b, i: (b, i, 0)),
        out_shape=jax.ShapeDtypeStruct((nb, t, d), BF16),
        compiler_params=_cparams("arbitrary", "arbitrary"),
        name="modulate",
    )(x3, sc, sh)


def _cast_rows(k):
    return _pick(k, 256)


def _mm_kernel(a_ref, w_ref, o_ref, wbf_ref):
    @pl.when(pl.program_id(1) == 0)
    def _():
        wbf_ref[...] = w_ref[...].astype(BF16)

    o_ref[...] = jnp.dot(a_ref[...].astype(BF16), wbf_ref[...], preferred_element_type=F32).astype(o_ref.dtype)


def matmul(a, w3, layer, *, col0=0, ncols=None, out_dtype=F32, bm=1024, bn=512):
    m, k = a.shape
    ncols = w3.shape[2] - col0 if ncols is None else ncols
    bm = _pick(m, bm)
    bn = _pick(ncols, bn)
    assert col0 % bn == 0 and ncols % bn == 0 and m % bm == 0
    off = col0 // bn
    return pl.pallas_call(
        _mm_kernel,
        grid=(ncols // bn, m // bm),
        in_specs=[
            pl.BlockSpec((bm, k), lambda j, i: (i, 0)),
            pl.BlockSpec((None, k, bn), lambda j, i: (layer, 0, j + off)),
        ],
        out_specs=pl.BlockSpec((bm, bn), lambda j, i: (i, j)),
        out_shape=jax.ShapeDtypeStruct((m, ncols), out_dtype),
        scratch_shapes=[pltpu.VMEM((k, bn), BF16)],
        compiler_params=_cparams("arbitrary", "arbitrary"),
        name="matmul",
    )(a, w3)


def _mm_shift_kernel(a_ref, w_ref, wt_ref, o_ref, wbf_ref, *, shift, rows):
    @pl.when(pl.program_id(1) == 0)
    def _():
        def body(r, c):
            sl = pl.ds(pl.multiple_of(r * rows, rows), rows)
            blk = jnp.concatenate([w_ref[sl, shift:], wt_ref[sl, :shift]], axis=1)
            wbf_ref[sl, :] = blk.astype(BF16)
            return c

        lax.fori_loop(0, w_ref.shape[0] // rows, body, 0)

    o_ref[...] = jnp.dot(a_ref[...].astype(BF16), wbf_ref[...], preferred_element_type=F32).astype(o_ref.dtype)


def matmul_shifted(a, w3, layer, *, col0, shift, ncols, out_dtype=F32, bm=1024, bn=512):
    m, k = a.shape
    bm = _pick(m, bm)
    bn = _pick(ncols, bn)
    assert col0 % bn == 0 and ncols % bn == 0 and m % bm == 0 and bn % LANES == 0 and 0 < shift < LANES
    off = col0 // bn
    tail0 = col0 // LANES
    per = bn // LANES
    kern = functools.partial(_mm_shift_kernel, shift=shift, rows=_cast_rows(k))
    return pl.pallas_call(
        kern,
        grid=(ncols // bn, m // bm),
        in_specs=[
            pl.BlockSpec((bm, k), lambda j, i: (i, 0)),
            pl.BlockSpec((None, k, bn), lambda j, i: (layer, 0, j + off)),
            pl.BlockSpec((None, k, LANES), lambda j, i: (layer, 0, tail0 + (j + 1) * per)),
        ],
        out_specs=pl.BlockSpec((bm, bn), lambda j, i: (i, j)),
        out_shape=jax.ShapeDtypeStruct((m, ncols), out_dtype),
        scratch_shapes=[pltpu.VMEM((k, bn), BF16)],
        compiler_params=_cparams("arbitrary", "arbitrary"),
        name="matmul_shifted",
    )(a, w3, w3)


def _forget_kernel(a_ref, w_ref, b_ref, lf_ref, fc_ref, carry_ref, *, cumsum):
    t = pl.program_id(1)
    z = jnp.dot(a_ref[...].astype(BF16), w_ref[...].astype(BF16), preferred_element_type=F32) + b_ref[...]
    lf = jnp.minimum(z, 0.0) - jnp.log1p(jnp.exp(-jnp.abs(z)))
    lf_ref[...] = lf
    if not cumsum:
        fc_ref[...] = lf
        return

    @pl.when(t == 0)
    def _():
        carry_ref[...] = jnp.zeros_like(carry_ref)

    bt = lf.shape[0]
    row = lax.broadcasted_iota(I32, lf.shape, 0)
    acc = lf
    s = 1
    while s < bt:
        acc = acc + jnp.where(row >= s, pltpu.roll(acc, s, axis=0), 0.0)
        s *= 2
    acc = acc + carry_ref[...]
    fc_ref[...] = acc
    carry_ref[...] = acc[bt - 1:bt, :]


def forget_gate(xin, w_in, layer, col0, b_pad, nb, t, cumsum):
    n, d = xin.shape
    bt = _pick(t, 512)
    nt = t // bt
    kern = functools.partial(_forget_kernel, cumsum=cumsum)
    return pl.pallas_call(
        kern,
        grid=(nb, nt),
        in_specs=[
            pl.BlockSpec((bt, d), lambda b, i: (b * nt + i, 0)),
            pl.BlockSpec((None, d, LANES), lambda b, i: (layer, 0, col0 // LANES)),
            pl.BlockSpec((1, LANES), lambda b, i: (0, 0)),
        ],
        out_specs=[pl.BlockSpec((bt, LANES), lambda b, i: (b * nt + i, 0))] * 2,
        out_shape=[jax.ShapeDtypeStruct((n, LANES), F32)] * 2,
        scratch_shapes=[pltpu.VMEM((1, LANES), F32)],
        compiler_params=_cparams("arbitrary", "arbitrary"),
        name="forget_gate",
    )(xin, w_in, b_pad)


def _flash_kernel(q_ref, k_ref, v_ref, fq_ref, fk_ref, o_ref, m_ref, l_ref, acc_ref, fqc_ref, *, scale):
    h = pl.program_id(1)
    qi = pl.program_id(2)
    ki = pl.program_id(3)
    bq = q_ref.shape[0]
    bk = k_ref.shape[0]

    @pl.when(ki == 0)
    def _():
        m_ref[...] = jnp.full_like(m_ref, -jnp.inf)
        l_ref[...] = jnp.zeros_like(l_ref)
        acc_ref[...] = jnp.zeros_like(acc_ref)
        lane = lax.broadcasted_iota(I32, fq_ref.shape, 1)
        fqc_ref[...] = jnp.sum(jnp.where(lane == h, fq_ref[...], 0.0), axis=1, keepdims=True)

    @pl.when(ki <= qi)
    def _():
        s = lax.dot_general(q_ref[...].astype(BF16), k_ref[...].astype(BF16), NT_DIMS,
                            preferred_element_type=F32) * scale
        s = s + fqc_ref[...] - fk_ref[pl.ds(h, 1), :]
        row = qi * bq + lax.broadcasted_iota(I32, s.shape, 0)
        col = ki * bk + lax.broadcasted_iota(I32, s.shape, 1)
        s = jnp.where(col <= row, s, -jnp.inf)
        m_prev = m_ref[...]
        m_new = jnp.maximum(m_prev, jnp.max(s, axis=1, keepdims=True))
        alpha = jnp.exp(m_prev - m_new)
        p = jnp.exp(s - m_new)
        l_ref[...] = alpha * l_ref[...] + jnp.sum(p, axis=1, keepdims=True)
        acc_ref[...] = alpha * acc_ref[...] + jnp.dot(p.astype(BF16), v_ref[...].astype(BF16),
                                                      preferred_element_type=F32)
        m_ref[...] = m_new

    @pl.when(ki == qi)
    def _():
        o_ref[...] = (acc_ref[...] / l_ref[...]).astype(o_ref.dtype)


def flash_attention(zqkv, fcum, fcum_t, nb, t, heads, dh):
    n = nb * t
    bq = _pick(t, 512)
    nq = t // bq
    kern = functools.partial(_flash_kernel, scale=dh ** -0.5)
    return pl.pallas_call(
        kern,
        grid=(nb, heads, nq, nq),
        in_specs=[
            pl.BlockSpec((bq, dh), lambda b, h, i, j: (b * nq + i, h)),
            pl.BlockSpec((bq, dh), lambda b, h, i, j: (b * nq + jnp.minimum(i, j), heads + h)),
            pl.BlockSpec((bq, dh), lambda b, h, i, j: (b * nq + jnp.minimum(i, j), 2 * heads + h)),
            pl.BlockSpec((bq, LANES), lambda b, h, i, j: (b * nq + i, 0)),
            pl.BlockSpec((None, heads, bq), lambda b, h, i, j: (b, 0, jnp.minimum(i, j))),
        ],
        out_specs=pl.BlockSpec((bq, dh), lambda b, h, i, j: (b * nq + i, h)),
        out_shape=jax.ShapeDtypeStruct((n, heads * dh), BF16),
        scratch_shapes=[pltpu.VMEM((bq, 1), F32), pltpu.VMEM((bq, 1), F32), pltpu.VMEM((bq, dh), F32),
                        pltpu.VMEM((bq, 1), F32)],
        compiler_params=_cparams("arbitrary", "arbitrary", "arbitrary", "arbitrary"),
        name="flash_attention",
    )(zqkv, zqkv, zqkv, fcum, fcum_t)


def _decode_kernel(pt_ref, q_ref, kn_ref, vn_ref, ln_ref, lp_ref, k_ref, v_ref, o_ref, m_ref, l_ref, acc_ref, car_ref,
                   *, scale, n_pages):
    del pt_ref
    i = pl.program_id(1)

    @pl.when(i == 0)
    def _():
        m_ref[...] = jnp.full_like(m_ref, -jnp.inf)
        l_ref[...] = jnp.zeros_like(l_ref)
        acc_ref[...] = jnp.zeros_like(acc_ref)
        car_ref[...] = ln_ref[...]

    q = q_ref[...]
    lp = lp_ref[...]
    page = lp.shape[0]
    r = lax.broadcasted_iota(I32, (page, page), 0)
    c = lax.broadcasted_iota(I32, (page, page), 1)
    after = jnp.where(c > r, 1.0, 0.0).astype(F32)
    suffix = jnp.dot(after, lp, preferred_element_type=F32, precision=lax.Precision.HIGHEST) + car_ref[...]
    car_ref[...] = suffix[0:1, :] + lp[0:1, :]
    s = jnp.sum(k_ref[...] * q[None], axis=-1, keepdims=True) * scale + suffix[:, :, None]
    m_prev = m_ref[...]
    m_new = jnp.maximum(m_prev, jnp.max(s, axis=0))
    alpha = jnp.exp(m_prev - m_new)
    p = jnp.exp(s - m_new[None])
    l_ref[...] = alpha * l_ref[...] + jnp.sum(p, axis=0)
    acc_ref[...] = alpha * acc_ref[...] + jnp.sum(p * v_ref[...], axis=0)
    m_ref[...] = m_new

    @pl.when(i == n_pages - 1)
    def _():
        s_new = jnp.sum(q * kn_ref[...], axis=-1, keepdims=True) * scale
        m_old = m_ref[...]
        m_fin = jnp.maximum(m_old, s_new)
        a = jnp.exp(m_old - m_fin)
        p_new = jnp.exp(s_new - m_fin)
        o_ref[...] = (a * acc_ref[...] + p_new * vn_ref[...]) / (a * l_ref[...] + p_new)


def decode_attention(q, k_new, v_new, logf_new, cache_k, cache_v, cache_logf, page_table, layer):
    nb, heads, dh = q.shape
    page = cache_k.shape[2]
    n_pages = page_table.shape[1]
    pt = page_table.reshape(-1).astype(I32)

    def page_idx(b, i, pt_ref):
        return pt_ref[b * n_pages + (n_pages - 1 - i)]

    kern = functools.partial(_decode_kernel, scale=dh ** -0.5, n_pages=n_pages)
    tok = pl.BlockSpec((None, heads, dh), lambda b, i, pt_ref: (b, 0, 0))
    return pl.pallas_call(
        kern,
        grid_spec=pltpu.PrefetchScalarGridSpec(
            num_scalar_prefetch=1,
            grid=(nb, n_pages),
            in_specs=[
                tok, tok, tok,
                pl.BlockSpec((None, 1, heads), lambda b, i, pt_ref: (b, 0, 0)),
                pl.BlockSpec((None, None, page, heads), lambda b, i, pt_ref: (layer, page_idx(b, i, pt_ref), 0, 0)),
                pl.BlockSpec((None, None, page, heads, dh),
                             lambda b, i, pt_ref: (layer, page_idx(b, i, pt_ref), 0, 0, 0)),
                pl.BlockSpec((None, None, page, heads, dh),
                             lambda b, i, pt_ref: (layer, page_idx(b, i, pt_ref), 0, 0, 0)),
            ],
            out_specs=tok,
            scratch_shapes=[pltpu.VMEM((heads, 1), F32), pltpu.VMEM((heads, 1), F32), pltpu.VMEM((heads, dh), F32),
                            pltpu.VMEM((1, heads), F32)],
        ),
        out_shape=jax.ShapeDtypeStruct((nb, heads, dh), F32),
        compiler_params=_cparams("arbitrary", "arbitrary"),
        name="decode_attention",
    )(pt, q, k_new, v_new, logf_new, cache_logf, cache_k, cache_v)


def _conv_kernel(b_ref, c_ref, x_ref, w_ref, prev_ref, y_ref, st_ref, car_ref):
    t = pl.program_id(2)
    v = c_ref[...] * x_ref[...]
    bt = v.shape[0]

    @pl.when(t == 0)
    def _():
        car_ref[...] = prev_ref[0]

    p0 = car_ref[0:1, :]
    p1 = car_ref[1:2, :]
    row = lax.broadcasted_iota(I32, v.shape, 0)
    r1 = jnp.where(row == 0, p1, pltpu.roll(v, 1, axis=0))
    r2 = jnp.where(row == 0, p0, jnp.where(row == 1, p1, pltpu.roll(v, 2, axis=0)))
    w = w_ref[...]
    y = w[0:1, :] * r2 + w[1:2, :] * r1 + w[2:3, :] * v
    y_ref[...] = (b_ref[...] * y).astype(y_ref.dtype)
    tail = v[bt - 2:bt, :]
    car_ref[...] = tail
    st_ref[0] = tail


def short_conv_prompt(zrest, col_b, conv_w_l, prev, nb, t, cw):
    n = nb * t
    bt = _pick(t, 512)
    bc = _pick(cw, 512)
    nt = t // bt
    o = col_b // bc
    per = cw // bc
    zspec = lambda k: pl.BlockSpec((bt, bc), lambda b, c, i: (b * nt + i, o + k * per + c))
    return pl.pallas_call(
        _conv_kernel,
        grid=(nb, per, nt),
        in_specs=[zspec(0), zspec(1), zspec(2),
                  pl.BlockSpec((3, bc), lambda b, c, i: (0, c)),
                  pl.BlockSpec((1, 2, bc), lambda b, c, i: (b, 0, c))],
        out_specs=[pl.BlockSpec((bt, bc), lambda b, c, i: (b * nt + i, c)),
                   pl.BlockSpec((1, 2, bc), lambda b, c, i: (b, 0, c))],
        out_shape=[jax.ShapeDtypeStruct((n, cw), BF16), jax.ShapeDtypeStruct((nb, 2, cw), F32)],
        scratch_shapes=[pltpu.VMEM((2, bc), F32)],
        compiler_params=_cparams("arbitrary", "arbitrary", "arbitrary"),
        name="short_conv_prompt",
    )(zrest, zrest, zrest, conv_w_l, prev)


def _conv_step_kernel(b_ref, c_ref, x_ref, w_ref, p0_ref, p1_ref, y_ref, v_ref):
    v = c_ref[...] * x_ref[...]
    w = w_ref[...]
    y = w[0:1, :] * p0_ref[...] + w[1:2, :] * p1_ref[...] + w[2:3, :] * v
    y_ref[...] = (b_ref[...] * y).astype(y_ref.dtype)
    v_ref[...] = v


def short_conv_step(zrest, col_b, conv_w_l, prev0, prev1, cw):
    nb = zrest.shape[0]
    bc = _pick(cw, 512)
    o = col_b // bc
    per = cw // bc
    zspec = lambda k: pl.BlockSpec((nb, bc), lambda c: (0, o + k * per + c))
    vec = pl.BlockSpec((nb, bc), lambda c: (0, c))
    return pl.pallas_call(
        _conv_step_kernel,
        grid=(per,),
        in_specs=[zspec(0), zspec(1), zspec(2), pl.BlockSpec((3, bc), lambda c: (0, c)), vec, vec],
        out_specs=[vec, vec],
        out_shape=[jax.ShapeDtypeStruct((nb, cw), BF16), jax.ShapeDtypeStruct((nb, cw), F32)],
        compiler_params=_cparams("arbitrary"),
        name="short_conv_step",
    )(zrest, zrest, zrest, conv_w_l, prev0, prev1)


def _ssm_weights_kernel(lre_ref, lim_ref, ldt_ref, bre_ref, bim_ref, cre_ref, cim_ref,
                        t_ref, s_ref, o_ref, al_ref, a1_ref, *, chunk):
    ns = lre_ref.shape[1]
    lr = lre_ref[...]
    li = lim_ref[...]
    dt = jnp.exp(ldt_ref[...])
    mag = jnp.exp(lr * dt)
    ang = li * dt
    ar = mag * jnp.cos(ang)
    ai = mag * jnp.sin(ang)
    den = lr * lr + li * li
    nr = ar - 1.0
    fr = (nr * lr + ai * li) / den
    fi = (ai * lr - nr * li) / den
    bre = bre_ref[...]
    bim = bim_ref[...]
    bbr = fr * bre - fi * bim
    bbi = fr * bim + fi * bre
    cre = cre_ref[...]
    cim = cim_ref[...]
    cfull = jnp.concatenate([cre, -cim], axis=1)
    pows = []
    pr = jnp.ones_like(ar)
    pi = jnp.zeros_like(ar)
    for _ in range(chunk + 1):
        pows.append((pr, pi))
        pr, pi = pr * ar - pi * ai, pr * ai + pi * ar
    a1_ref[...] = jnp.concatenate([ar, ai], axis=1)
    al_ref[...] = jnp.concatenate(list(pows[chunk]), axis=1)
    taps = [None] * chunk
    for j in range(chunk):
        qr, qi = pows[chunk - 1 - j]
        blk = jnp.concatenate([qr * bbr - qi * bbi, qr * bbi + qi * bbr], axis=1)
        s_ref[j * LANES:(j + 1) * LANES, :] = blk.astype(s_ref.dtype)
        taps[chunk - 1 - j] = lax.dot_general(blk, cfull, NT_DIMS, preferred_element_type=F32,
                                              precision=lax.Precision.HIGHEST)
    zero = jnp.zeros((LANES, LANES), F32)
    for j in range(chunk):
        for i in range(chunk):
            t_ref[j * LANES:(j + 1) * LANES, i * LANES:(i + 1) * LANES] = (
                taps[i - j] if i >= j else zero).astype(t_ref.dtype)
    for i in range(chunk):
        qr, qi = pows[i + 1]
        o_ref[i * LANES:(i + 1) * LANES, :] = jnp.concatenate(
            [cre * qr - cim * qi, -cre * qi - cim * qr], axis=1).astype(o_ref.dtype)


def ssm_weights(lam_re, lam_im, log_dt, b_re, b_im, c_re, c_im, chunk):
    g, p = lam_re.shape
    gc = b_re.shape[-1]
    gpt = LANES // gc
    nt = g // gpt
    ns = gpt * p
    eye = jnp.eye(gpt, dtype=F32)
    tile = lambda x: x.reshape(nt, 1, ns)
    ldt = tile(jnp.broadcast_to(log_dt[:, None], (g, p)))
    bdiag = lambda b: jnp.einsum('tgpc,gh->thcgp', b.reshape(nt, gpt, p, gc), eye).reshape(nt, LANES, ns)
    cdiag = lambda c: jnp.einsum('tgcp,gh->thcgp', c.reshape(nt, gpt, gc, p), eye).reshape(nt, LANES, ns)
    row = pl.BlockSpec((None, 1, ns), lambda j: (j, 0, 0))
    mat = pl.BlockSpec((None, LANES, ns), lambda j: (j, 0, 0))
    lc = chunk * LANES
    kern = functools.partial(_ssm_weights_kernel, chunk=chunk)
    return pl.pallas_call(
        kern,
        grid=(nt,),
        in_specs=[row, row, row, mat, mat, mat, mat],
        out_specs=[pl.BlockSpec((None, lc, lc), lambda j: (j, 0, 0)),
                   pl.BlockSpec((None, lc, 2 * ns), lambda j: (j, 0, 0)),
                   pl.BlockSpec((None, lc, 2 * ns), lambda j: (j, 0, 0)),
                   pl.BlockSpec((None, 1, 2 * ns), lambda j: (j, 0, 0)),
                   pl.BlockSpec((None, 1, 2 * ns), lambda j: (j, 0, 0))],
        out_shape=[jax.ShapeDtypeStruct((nt, lc, lc), BF16),
                   jax.ShapeDtypeStruct((nt, lc, 2 * ns), BF16),
                   jax.ShapeDtypeStruct((nt, lc, 2 * ns), BF16),
                   jax.ShapeDtypeStruct((nt, 1, 2 * ns), F32),
                   jax.ShapeDtypeStruct((nt, 1, 2 * ns), F32)],
        compiler_params=_cparams("arbitrary"),
        name="ssm_weights",
    )(tile(lam_re), tile(lam_im), ldt, bdiag(b_re), bdiag(b_im), cdiag(c_re), cdiag(c_im))


def _ssm_kernel(u_ref, h0_ref, t_ref, s_ref, o_ref, al_ref, d_ref, z_ref, hf_ref, ucat_ref, sloc_ref, sprev_ref,
                *, chunk):
    nb, t, _ = u_ref.shape
    tc = t // chunk
    ns = al_ref.shape[1] // 2
    for b in range(nb):
        for j in range(chunk):
            ucat_ref[b * tc:(b + 1) * tc, j * LANES:(j + 1) * LANES] = u_ref[b, pl.ds(j, tc, stride=chunk), :]
    nc = ns // LANES
    for b in range(nb):
        rows = slice(b * tc, (b + 1) * tc)
        sl = jnp.dot(ucat_ref[rows, :].astype(BF16), s_ref[...], preferred_element_type=F32)
        for c in range(2 * nc):
            sloc_ref[c, rows, :] = sl[:, c * LANES:(c + 1) * LANES]

    def step(k, h):
        at = pl.ds(k, nb, stride=tc)
        new = [None] * (2 * nc)
        for c in range(nc):
            lanes = slice(c * LANES, (c + 1) * LANES)
            alr = al_ref[:, lanes]
            ali = al_ref[:, ns + c * LANES:ns + (c + 1) * LANES]
            hr = h[c]
            hi = h[nc + c]
            sprev_ref[c, at, :] = hr
            sprev_ref[nc + c, at, :] = hi
            new[c] = alr * hr - ali * hi + sloc_ref[c, at, :]
            new[nc + c] = alr * hi + ali * hr + sloc_ref[nc + c, at, :]
        return tuple(new)

    h0 = h0_ref[...]
    hfin = lax.fori_loop(0, tc, step, tuple(h0[:, c * LANES:(c + 1) * LANES] for c in range(2 * nc)))
    hf_ref[...] = jnp.concatenate(hfin, axis=1)
    for b in range(nb):
        rows = slice(b * tc, (b + 1) * tc)
        uc = ucat_ref[rows, :]
        sp = jnp.concatenate([sprev_ref[c, rows, :] for c in range(2 * nc)], axis=1)
        y = jnp.dot(uc.astype(BF16), t_ref[...], preferred_element_type=F32)
        y = y + lax.dot_general(sp.astype(BF16), o_ref[...], NT_DIMS, preferred_element_type=F32)
        z = jax.nn.gelu(y + d_ref[...] * uc)
        for i in range(chunk):
            z_ref[b, pl.ds(i, tc, stride=chunk), :] = z[:, i * LANES:(i + 1) * LANES]


def ssm_prompt(zrest3, col_u, wts, h0cat, d_cat, chunk):
    tmat, smat, omat, al, _ = wts
    nb, t, _ = zrest3.shape
    nt, lc, ns2 = smat.shape
    tc = t // chunk
    kern = functools.partial(_ssm_kernel, chunk=chunk)
    return pl.pallas_call(
        kern,
        grid=(nt,),
        in_specs=[pl.BlockSpec((nb, t, LANES), lambda j: (0, 0, col_u // LANES + j)),
                  pl.BlockSpec((None, nb, ns2), lambda j: (j, 0, 0)),
                  pl.BlockSpec((None, lc, lc), lambda j: (j, 0, 0)),
                  pl.BlockSpec((None, lc, ns2), lambda j: (j, 0, 0)),
                  pl.BlockSpec((None, lc, ns2), lambda j: (j, 0, 0)),
                  pl.BlockSpec((None, 1, ns2), lambda j: (j, 0, 0)),
                  pl.BlockSpec((None, 1, lc), lambda j: (j, 0, 0))],
        out_specs=[pl.BlockSpec((nb, t, LANES), lambda j: (0, 0, j)),
                   pl.BlockSpec((None, nb, ns2), lambda j: (j, 0, 0))],
        out_shape=[jax.ShapeDtypeStruct((nb, t, nt * LANES), F32), jax.ShapeDtypeStruct((nt, nb, ns2), F32)],
        scratch_shapes=[pltpu.VMEM((nb * tc, lc), F32), pltpu.VMEM((ns2 // LANES, nb * tc, LANES), F32),
                        pltpu.VMEM((ns2 // LANES, nb * tc, LANES), F32)],
        compiler_params=_cparams("arbitrary"),
        name="ssm_prompt",
    )(zrest3, h0cat, tmat, smat, omat, al, d_cat)


def _ssm_step_kernel(u_ref, h0_ref, s_ref, o_ref, a1_ref, d_ref, z_ref, h_ref, *, chunk):
    ns = a1_ref.shape[1] // 2
    u = u_ref[...]
    x = jnp.dot(u.astype(BF16), s_ref[(chunk - 1) * LANES:chunk * LANES, :], preferred_element_type=F32)
    ar = a1_ref[:, :ns]
    ai = a1_ref[:, ns:]
    h0 = h0_ref[...]
    hr = ar * h0[:, :ns] - ai * h0[:, ns:] + x[:, :ns]
    hi = ar * h0[:, ns:] + ai * h0[:, :ns] + x[:, ns:]
    h = jnp.concatenate([hr, hi], axis=1)
    h_ref[...] = h
    y = lax.dot_general(h.astype(BF16), o_ref[...], NT_DIMS, preferred_element_type=F32)
    z_ref[...] = jax.nn.gelu(y + d_ref[...] * u)


def ssm_step(zrest, col_u, wts, cmat, h0cat, d_t, chunk):
    _, smat, _, _, a1 = wts
    nb = zrest.shape[0]
    nt, lc, ns2 = smat.shape
    kern = functools.partial(_ssm_step_kernel, chunk=chunk)
    return pl.pallas_call(
        kern,
        grid=(nt,),
        in_specs=[pl.BlockSpec((nb, LANES), lambda j: (0, col_u // LANES + j)),
                  pl.BlockSpec((None, nb, ns2), lambda j: (j, 0, 0)),
                  pl.BlockSpec((None, lc, ns2), lambda j: (j, 0, 0)),
                  pl.BlockSpec((None, LANES, ns2), lambda j: (j, 0, 0)),
                  pl.BlockSpec((None, 1, ns2), lambda j: (j, 0, 0)),
                  pl.BlockSpec((None, 1, LANES), lambda j: (j, 0, 0))],
        out_specs=[pl.BlockSpec((nb, LANES), lambda j: (0, j)),
                   pl.BlockSpec((None, nb, ns2), lambda j: (j, 0, 0))],
        out_shape=[jax.ShapeDtypeStruct((nb, nt * LANES), F32), jax.ShapeDtypeStruct((nt, nb, ns2), F32)],
        compiler_params=_cparams("arbitrary"),
        name="ssm_step",
    )(zrest, h0cat, smat, cmat, a1, d_t)


def _glu_kernel(a_ref, zt_ref, w_ref, o_ref, wbf_ref):
    @pl.when(pl.program_id(1) == 0)
    def _():
        wbf_ref[...] = w_ref[...].astype(BF16)

    acc = jnp.dot(a_ref[...].astype(BF16), wbf_ref[...], preferred_element_type=F32)
    o_ref[...] = (zt_ref[...] * jax.nn.sigmoid(acc)).astype(o_ref.dtype)


def glu(z, w3, layer):
    m, k = z.shape
    bm = _pick(m, 1024)
    bn = _pick(k, 512)
    return pl.pallas_call(
        _glu_kernel,
        grid=(k // bn, m // bm),
        in_specs=[pl.BlockSpec((bm, k), lambda j, i: (i, 0)),
                  pl.BlockSpec((bm, bn), lambda j, i: (i, j)),
                  pl.BlockSpec((None, k, bn), lambda j, i: (layer, 0, j))],
        out_specs=pl.BlockSpec((bm, bn), lambda j, i: (i, j)),
        out_shape=jax.ShapeDtypeStruct((m, k), BF16),
        scratch_shapes=[pltpu.VMEM((k, bn), BF16)],
        compiler_params=_cparams("arbitrary", "arbitrary"),
        name="glu",
    )(z, z, w3)


def _merge_kernel(ya_ref, yc_ref, ys_ref, ga_ref, gc_ref, gs_ref, wa_ref, wc_ref, ws_ref, o_ref,
                  wa_bf, wc_bf, ws_bf):
    @pl.when(pl.program_id(1) == 0)
    def _():
        wa_bf[...] = wa_ref[...].astype(BF16)
        wc_bf[...] = wc_ref[...].astype(BF16)
        ws_bf[...] = ws_ref[...].astype(BF16)

    acc = jax.nn.sigmoid(ga_ref[...]) * jnp.dot(ya_ref[...], wa_bf[...], preferred_element_type=F32)
    acc = acc + jax.nn.sigmoid(gc_ref[...]) * jnp.dot(yc_ref[...], wc_bf[...], preferred_element_type=F32)
    acc = acc + jax.nn.sigmoid(gs_ref[...]) * jnp.dot(ys_ref[...], ws_bf[...], preferred_element_type=F32)
    o_ref[...] = acc.astype(o_ref.dtype)


def merge_branches(ya, yc, ys, zrest, col_g, w_a, w_c, w_s, layer, d):
    m = ya.shape[0]
    bm = _pick(m, 512)
    bn = _pick(d, 512)
    og = col_g // bn
    per = d // bn
    yspec = lambda y: pl.BlockSpec((bm, y.shape[1]), lambda j, i: (i, 0))
    gspec = lambda k: pl.BlockSpec((bm, bn), lambda j, i: (i, og + k * per + j))
    wspec = lambda w: pl.BlockSpec((None, w.shape[1], bn), lambda j, i: (layer, 0, j))
    return pl.pallas_call(
        _merge_kernel,
        grid=(per, m // bm),
        in_specs=[yspec(ya), yspec(yc), yspec(ys), gspec(0), gspec(1), gspec(2), wspec(w_a), wspec(w_c), wspec(w_s)],
        out_specs=pl.BlockSpec((bm, bn), lambda j, i: (i, j)),
        out_shape=jax.ShapeDtypeStruct((m, d), BF16),
        scratch_shapes=[pltpu.VMEM((w_a.shape[1], bn), BF16), pltpu.VMEM((w_c.shape[1], bn), BF16),
                        pltpu.VMEM((w_s.shape[1], bn), BF16)],
        compiler_params=_cparams("arbitrary", "arbitrary"),
        name="merge_branches",
    )(ya, yc, ys, zrest, zrest, zrest, w_a, w_c, w_s)


def _layer_norm(v, g, b):
    mu = jnp.mean(v, axis=-1, keepdims=True)
    var = jnp.mean(jnp.square(v - mu), axis=-1, keepdims=True)
    return (v - mu) * lax.rsqrt(var + LN_EPS) * g + b


def _ln_route_kernel(x_ref, y_ref, gate_ref, g_ref, b_ref, sc_ref, sh_ref, wr_ref, br_ref,
                     x1_ref, tok_ref, eidx_ref, wts_ref, *, alpha, n_groups, per_group):
    x1 = _layer_norm(alpha * x_ref[0] + (1.0 + gate_ref[0]) * y_ref[0], g_ref[...], b_ref[...])
    x1_ref[0] = x1
    tok = x1 * (1.0 + sc_ref[0]) + sh_ref[0]
    tok_ref[0] = tok
    logit = jnp.dot(tok, wr_ref[...], preferred_element_type=F32, precision=lax.Precision.HIGHEST) + br_ref[...]
    lane_i = lax.broadcasted_iota(I32, logit.shape, 1)
    lane = lane_i.astype(F32)
    big = float(LANES)
    neg = -jnp.inf
    gl = jnp.where(lane < n_groups, logit, neg)
    gmax = jnp.max(gl, axis=1, keepdims=True)
    gidx = jnp.min(jnp.where(gl == gmax, lane, big), axis=1, keepdims=True)
    gprob = 1.0 / jnp.sum(jnp.exp(gl - gmax), axis=1, keepdims=True)
    lo = n_groups + gidx * per_group
    el = jnp.where((lane >= lo) & (lane < lo + per_group), logit, neg)
    t1 = jnp.max(el, axis=1, keepdims=True)
    i1 = jnp.min(jnp.where(el == t1, lane, big), axis=1, keepdims=True)
    el2 = jnp.where(lane == i1, neg, el)
    t2 = jnp.max(el2, axis=1, keepdims=True)
    i2 = jnp.min(jnp.where(el2 == t2, lane, big), axis=1, keepdims=True)
    e2 = jnp.exp(t2 - t1)
    w1 = gprob / (1.0 + e2)
    w2 = gprob * e2 / (1.0 + e2)
    eidx_ref[0] = jnp.where(lane_i == 0, i1 - n_groups, jnp.where(lane_i == 1, i2 - n_groups, 0.0)).astype(I32)
    wts_ref[0] = jnp.where(lane_i == 0, w1, jnp.where(lane_i == 1, w2, 0.0))


def ln_route(x3, y3, gate, ln_g, ln_b, sc, sh, w_router, b_router, alpha, n_groups, per_group):
    nb, t, d = x3.shape
    bt = _pick(t, 256)
    blk = pl.BlockSpec((1, bt, d), lambda b, i: (b, i, 0))
    row = pl.BlockSpec((1, d), lambda b, i: (0, 0))
    sel = pl.BlockSpec((1, bt, LANES), lambda b, i: (b, i, 0))
    kern = functools.partial(_ln_route_kernel, alpha=alpha, n_groups=n_groups, per_group=per_group)
    return pl.pallas_call(
        kern,
        grid=(nb, t // bt),
        in_specs=[blk, blk, _mod_spec(gate, bt), row, row, _mod_spec(sc, bt), _mod_spec(sh, bt),
                  pl.BlockSpec((d, LANES), lambda b, i: (0, 0)), pl.BlockSpec((1, LANES), lambda b, i: (0, 0))],
        out_specs=[blk, blk, sel, sel],
        out_shape=[jax.ShapeDtypeStruct((nb, t, d), F32), jax.ShapeDtypeStruct((nb, t, d), F32),
                   jax.ShapeDtypeStruct((nb, t, LANES), I32), jax.ShapeDtypeStruct((nb, t, LANES), F32)],
        compiler_params=_cparams("arbitrary", "arbitrary"),
        name="ln_route",
    )(x3, y3, gate, ln_g, ln_b, sc, sh, w_router, b_router)


def _ln_combine_kernel(x_ref, y0_ref, y1_ref, wts_ref, gate_ref, g_ref, b_ref, o_ref, *, alpha):
    w = wts_ref[0]
    ffn = w[:, 0:1] * y0_ref[0] + w[:, 1:2] * y1_ref[0]
    o_ref[0] = _layer_norm(alpha * x_ref[0] + (1.0 + gate_ref[0]) * ffn, g_ref[...], b_ref[...])


def ln_combine(x3, y_assign, wts, gate, ln_g, ln_b, alpha):
    nb, t, d = x3.shape
    bt = _pick(t, 256)
    blk = pl.BlockSpec((1, bt, d), lambda b, i: (b, i, 0))
    row = pl.BlockSpec((1, d), lambda b, i: (0, 0))
    ysp = lambda k: pl.BlockSpec((None, 1, bt, d), lambda b, i: (k, b, i, 0))
    kern = functools.partial(_ln_combine_kernel, alpha=alpha)
    return pl.pallas_call(
        kern,
        grid=(nb, t // bt),
        in_specs=[blk, ysp(0), ysp(1), pl.BlockSpec((1, bt, LANES), lambda b, i: (b, i, 0)),
                  _mod_spec(gate, bt), row, row],
        out_specs=blk,
        out_shape=jax.ShapeDtypeStruct((nb, t, d), F32),
        compiler_params=_cparams("arbitrary", "arbitrary"),
        name="ln_combine",
    )(x3, y_assign, y_assign, wts, gate, ln_g, ln_b)


def _rank_kernel(e_ref, rank_ref, cnt_ref, car_ref):
    first = (pl.program_id(0) == 0) & (pl.program_id(1) == 0)

    @pl.when(first)
    def _():
        car_ref[...] = jnp.zeros_like(car_ref)

    e = e_ref[...]
    rb = e.shape[1]
    ex = lax.broadcasted_iota(I32, (LANES, rb), 0)
    onehot = jnp.where(ex == e, 1.0, 0.0).astype(BF16)
    r = lax.broadcasted_iota(I32, (rb, rb), 0)
    c = lax.broadcasted_iota(I32, (rb, rb), 1)
    upto = jnp.where(r <= c, 1.0, 0.0).astype(BF16)
    cum = jnp.dot(onehot, upto, preferred_element_type=F32)
    oh = onehot.astype(F32)
    rank = jnp.sum(oh * (cum - 1.0 + car_ref[...]), axis=0, keepdims=True)
    rank_ref[...] = rank.astype(I32)
    car_ref[...] = car_ref[...] + jnp.sum(oh, axis=1, keepdims=True)
    cnt_ref[...] = jnp.broadcast_to(car_ref[...], cnt_ref.shape).astype(I32)


def expert_ranks(e_rows):
    two, nblk, _, rb = e_rows.shape
    return pl.pallas_call(
        _rank_kernel,
        grid=(two, nblk),
        in_specs=[pl.BlockSpec((None, None, 1, rb), lambda k, i: (k, i, 0, 0))],
        out_specs=[pl.BlockSpec((None, None, 1, rb), lambda k, i: (k, i, 0, 0)),
                   pl.BlockSpec((LANES, LANES), lambda k, i: (0, 0))],
        out_shape=[jax.ShapeDtypeStruct(e_rows.shape, I32), jax.ShapeDtypeStruct((LANES, LANES), I32)],
        scratch_shapes=[pltpu.VMEM((LANES, 1), F32)],
        compiler_params=_cparams("arbitrary", "arbitrary"),
        name="expert_ranks",
    )(e_rows)


def _move_rows_kernel(idx_ref, src_ref, *rest, per_step, n_src, scatter):
    dst_ref, sem = rest[-2:]
    base = pl.program_id(0) * per_step

    def copy(u):
        a = base + u
        if scatter:
            return pltpu.make_async_copy(src_ref.at[pl.ds(lax.rem(a, n_src), 1)], dst_ref.at[pl.ds(idx_ref[a], 1)],
                                         sem)
        return pltpu.make_async_copy(src_ref.at[pl.ds(idx_ref[a], 1)], dst_ref.at[pl.ds(a, 1)], sem)

    def start(u, c):
        copy(u).start()
        return c

    def wait(u, c):
        copy(u).wait()
        return c

    lax.fori_loop(0, per_step, start, 0)
    lax.fori_loop(0, per_step, wait, 0)


def move_rows(idx, src, n_dst, scatter):
    na = idx.shape[0]
    n_src, d = src.shape
    per_step = _pick(na, 256)
    kern = functools.partial(_move_rows_kernel, per_step=per_step, n_src=n_src, scatter=scatter)
    hbm = pl.BlockSpec(memory_space=pl.ANY)
    operands = (idx, src, jnp.zeros((n_dst, d), src.dtype)) if scatter else (idx, src)
    return pl.pallas_call(
        kern,
        grid_spec=pltpu.PrefetchScalarGridSpec(
            num_scalar_prefetch=1,
            grid=(na // per_step,),
            in_specs=[hbm] * (len(operands) - 1),
            out_specs=hbm,
            scratch_shapes=[pltpu.SemaphoreType.DMA(())],
        ),
        out_shape=jax.ShapeDtypeStruct((n_dst, d), src.dtype),
        input_output_aliases={2: 0} if scatter else {},
        compiler_params=_cparams("arbitrary"),
        name="move_rows_scatter" if scatter else "move_rows_gather",
    )(*operands)


def _expert_kernel(be_ref, nu_ref, x_ref, wg_ref, wu_ref, wd_ref, o_ref):
    del be_ref

    @pl.when(pl.program_id(0) < nu_ref[0])
    def _():
        xb = x_ref[...].astype(BF16)
        g = jnp.dot(xb, wg_ref[...], preferred_element_type=F32)
        u = jnp.dot(xb, wu_ref[...], preferred_element_type=F32)
        h = (g * jax.nn.sigmoid(g) * u).astype(BF16)
        o_ref[...] = jnp.dot(h, wd_ref[...], preferred_element_type=F32)

    @pl.when(pl.program_id(0) >= nu_ref[0])
    def _():
        o_ref[...] = jnp.zeros_like(o_ref)


def expert_blocks(x_sorted, blk_e, n_used, wg, wu, wd, rows):
    r, d = x_sorted.shape
    hid = wg.shape[-1]
    nblk = r // rows
    live = lambda i, nu: jnp.minimum(i, nu[0] - 1)
    return pl.pallas_call(
        _expert_kernel,
        grid_spec=pltpu.PrefetchScalarGridSpec(
            num_scalar_prefetch=2,
            grid=(nblk,),
            in_specs=[pl.BlockSpec((rows, d), lambda i, be, nu: (live(i, nu), 0)),
                      pl.BlockSpec((None, d, hid), lambda i, be, nu: (be[live(i, nu)], 0, 0)),
                      pl.BlockSpec((None, d, hid), lambda i, be, nu: (be[live(i, nu)], 0, 0)),
                      pl.BlockSpec((None, hid, d), lambda i, be, nu: (be[live(i, nu)], 0, 0))],
            out_specs=pl.BlockSpec((rows, d), lambda i, be, nu: (i, 0)),
        ),
        out_shape=jax.ShapeDtypeStruct((r, d), F32),
        compiler_params=_cparams("arbitrary"),
        name="expert_blocks",
    )(blk_e, n_used, x_sorted, wg, wu, wd)


def hierarchical_moe(tok3, eidx, wg, wu, wd, rows):
    nb, t, d = tok3.shape
    n = nb * t
    n_exp = wg.shape[0]
    e2 = eidx.reshape(n, LANES)[:, :2].T
    if n >= LANES:
        rb = _pick(n, 512)
        rank, cnt = expert_ranks(e2.reshape(2, n // rb, 1, rb))
        rank = rank.reshape(-1)
        counts = cnt[:n_exp, 0]
    else:
        ef = e2.reshape(-1)
        ar = jnp.arange(2 * n)
        rank = jnp.sum((ef[:, None] == ef[None, :]) & (ar[None, :] < ar[:, None]), axis=1).astype(I32)
        counts = jnp.sum(ef[:, None] == jnp.arange(n_exp)[None, :], axis=0).astype(I32)
    ef = e2.reshape(-1)
    nblk = min(-(-(2 * n + n_exp * (rows - 1)) // rows), 2 * n)
    padded = (counts + rows - 1) // rows * rows
    pad_end = jnp.cumsum(padded)
    dest = ((pad_end - padded)[ef] + rank).astype(I32)
    blk_e = jnp.minimum(jnp.searchsorted(pad_end, jnp.arange(nblk) * rows, side='right'), n_exp - 1).astype(I32)
    n_used = (pad_end[-1:] // rows).astype(I32)
    x_sorted = move_rows(dest, tok3.reshape(n, d), nblk * rows, scatter=True)
    y_rows = expert_blocks(x_sorted, blk_e, n_used, wg, wu, wd, rows)
    y_assign = move_rows(dest, y_rows, 2 * n, scatter=False)
    return y_assign.reshape(2, nb, t, d)


def _dims(p):
    d = p['w_o'].shape[-1]
    aw = p['w_br_attn'].shape[1]
    cw = p['w_br_conv'].shape[1]
    sw = p['w_br_ssm'].shape[1]
    return d, aw, cw, sw


def _mixer_common(xin, p, layer, nb, t, heads, cumsum):
    d, aw, cw, sw = _dims(p)
    rest = 3 * cw + sw + 3 * d
    zqkv = matmul(xin, p['w_in'], layer, col0=0, ncols=3 * aw)
    zrest = matmul_shifted(xin, p['w_in'], layer, col0=3 * aw, shift=heads, ncols=rest)
    b_pad = jnp.zeros((1, LANES), F32).at[0, :heads].set(p['b_forget'][layer])
    logf, fcum = forget_gate(xin, p['w_in'], layer, 3 * aw, b_pad, nb, t, cumsum)
    return zqkv, zrest, logf, fcum


def _finish_mixer(x3, ya, yc, ys, zrest, p, layer, mods, alpha, n_groups, per_group, w_router, b_router):
    d, aw, cw, sw = _dims(p)
    nb, t, _ = x3.shape
    merged = merge_branches(ya, yc, ys, zrest, 3 * cw + sw, p['w_br_attn'], p['w_br_conv'], p['w_br_ssm'], layer, d)
    mix = matmul(merged, p['w_o'], layer)
    sh1, sc1, g1, sh2, sc2, g2 = mods
    return ln_route(x3, mix.reshape(nb, t, d), g1, p['ln1_g'][layer][None], p['ln1_b'][layer][None], sc2, sh2,
                    w_router, b_router, alpha, n_groups, per_group)


def _ssm_tiles(state, nt):
    nb = state.shape[0]
    return state.reshape(nb, nt, -1).transpose(1, 0, 2)


def _ssm_untile(h, nb, g, p):
    return h.transpose(1, 0, 2).reshape(nb, g, p)


def kernel(x_prompt, x_sample, c_prompt, c_sample, cache_k, cache_v, cache_logf, page_table, state_conv, state_ssm_re, state_ssm_im, w_ada, b_ada, ln1_g, ln1_b, ln2_g, ln2_b, w_in, b_forget, conv_w, ssm_lambda_re, ssm_lambda_im, ssm_log_dt, ssm_b_re, ssm_b_im, ssm_c_re, ssm_c_im, ssm_d, ssm_w_glu, w_br_attn, w_br_conv, w_br_ssm, w_o, router_w_group, router_b_group, router_w_expert, router_b_expert, moe_w_gate, moe_w_up, moe_w_down):
    p = dict(w_in=w_in, b_forget=b_forget, w_br_attn=w_br_attn, w_br_conv=w_br_conv, w_br_ssm=w_br_ssm, w_o=w_o,
             ln1_g=ln1_g, ln1_b=ln1_b, ln2_g=ln2_g, ln2_b=ln2_b)
    depth = w_in.shape[0]
    nbp, t, d = x_prompt.shape
    nbs, ts, _ = x_sample.shape
    assert ts == 1
    heads, dh = cache_k.shape[3], cache_k.shape[4]
    aw = heads * dh
    cw = conv_w.shape[-1]
    sw = ssm_d.shape[-1]
    g, pst = ssm_lambda_re.shape[1], ssm_lambda_re.shape[2]
    gc = ssm_b_re.shape[-1]
    nt = g * gc // LANES
    ns = (LANES // gc) * pst
    n_groups = router_w_group.shape[-1]
    n_exp = router_w_expert.shape[-1]
    per_group = n_exp // n_groups
    alpha = (2 * depth) ** 0.25
    chunk = SSM_CHUNK
    col_conv = 0
    col_u = 3 * cw

    r = nbp + nbs
    rpad = -(-r // 8) * 8
    c_all = jnp.zeros((rpad, d), F32).at[:nbp].set(c_prompt).at[nbp:r].set(c_sample)
    mod_all = ada_modulation(c_all, w_ada, b_ada)

    wg_bf = moe_w_gate.astype(BF16)
    wu_bf = moe_w_up.astype(BF16)
    wd_bf = moe_w_down.astype(BF16)

    xp = x_prompt
    xs = x_sample.reshape(1, nbs, d)
    outs = {k: [] for k in ('kp', 'vp', 'fp', 'ks', 'vs', 'fs', 'cp', 'cs', 'srp', 'sip', 'srs', 'sis')}
    zero_conv = jnp.zeros((nbp, 2, cw), F32)
    zero_h = jnp.zeros((nt, nbp, 2 * ns), F32)
    for l in range(depth):
        mp = [mod_all[l, :nbp, i * d:(i + 1) * d].reshape(nbp, 1, d) for i in range(6)]
        ms = [mod_all[l, nbp:r, i * d:(i + 1) * d].reshape(1, nbs, d) for i in range(6)]
        w_router = jnp.zeros((d, LANES), F32).at[:, :n_groups].set(router_w_group[l]) \
            .at[:, n_groups:n_groups + n_exp].set(router_w_expert[l])
        b_router = jnp.zeros((1, LANES), F32).at[0, :n_groups].set(router_b_group[l]) \
            .at[0, n_groups:n_groups + n_exp].set(router_b_expert[l])
        wts = ssm_weights(ssm_lambda_re[l], ssm_lambda_im[l], ssm_log_dt[l], ssm_b_re[l], ssm_b_im[l],
                          ssm_c_re[l], ssm_c_im[l], chunk)
        d_t = ssm_d[l].reshape(nt, 1, LANES)
        d_cat = jnp.tile(d_t, (1, 1, chunk))
        eye = jnp.eye(LANES // gc, dtype=F32)
        cdiag = lambda c: jnp.einsum('tgcp,gh->thcgp', c.reshape(nt, LANES // gc, gc, pst), eye).reshape(nt, LANES, ns)
        cmat = jnp.concatenate([cdiag(ssm_c_re[l]), -cdiag(ssm_c_im[l])], axis=2).astype(BF16)

        n = nbp * t
        xin = modulate(xp, mp[1], mp[0]).reshape(n, d)
        zqkv, zrest, logf, fcum = _mixer_common(xin, p, l, nbp, t, heads, True)
        fcum_t = fcum[:, :heads].reshape(nbp, t, heads).transpose(0, 2, 1)
        ya = flash_attention(zqkv, fcum, fcum_t, nbp, t, heads, dh)
        yc, conv_p = short_conv_prompt(zrest, col_conv, conv_w[l], zero_conv, nbp, t, cw)
        zs, hfin = ssm_prompt(zrest.reshape(nbp, t, -1), col_u, wts, zero_h, d_cat, chunk)
        ys = glu(zs.reshape(n, sw), ssm_w_glu, l)
        x1, tok, eidx, wsel = _finish_mixer(xp, ya, yc, ys, zrest, p, l, mp, alpha, n_groups, per_group,
                                            w_router, b_router)
        y_assign = hierarchical_moe(tok, eidx, wg_bf[l], wu_bf[l], wd_bf[l], MOE_ROWS)
        xp = ln_combine(x1, y_assign, wsel, mp[5], ln2_g[l][None], ln2_b[l][None], alpha)
        outs['kp'].append(zqkv[:, aw:2 * aw].reshape(nbp, t, heads, dh))
        outs['vp'].append(zqkv[:, 2 * aw:].reshape(nbp, t, heads, dh))
        outs['fp'].append(logf[:, :heads].reshape(nbp, t, heads))
        outs['cp'].append(conv_p)
        outs['srp'].append(_ssm_untile(hfin[:, :, :ns], nbp, g, pst))
        outs['sip'].append(_ssm_untile(hfin[:, :, ns:], nbp, g, pst))

        xin_s = modulate(xs, ms[1], ms[0]).reshape(nbs, d)
        zqkv_s, zrest_s, logf_s, _ = _mixer_common(xin_s, p, l, 1, nbs, heads, False)
        q_s = zqkv_s[:, :aw].reshape(nbs, heads, dh)
        k_s = zqkv_s[:, aw:2 * aw].reshape(nbs, heads, dh)
        v_s = zqkv_s[:, 2 * aw:].reshape(nbs, heads, dh)
        ya_s = decode_attention(q_s, k_s, v_s, logf_s[:, :heads].reshape(nbs, 1, heads), cache_k, cache_v, cache_logf,
                                page_table, l)
        ya_s = ya_s.reshape(nbs, aw).astype(BF16)
        yc_s, v_row = short_conv_step(zrest_s, col_conv, conv_w[l], state_conv[l, :, 0], state_conv[l, :, 1], cw)
        h0 = jnp.concatenate([_ssm_tiles(state_ssm_re[l], nt), _ssm_tiles(state_ssm_im[l], nt)], axis=2)
        zs_s, h_s = ssm_step(zrest_s, col_u, wts, cmat, h0, d_t, chunk)
        ys_s = glu(zs_s, ssm_w_glu, l)
        x1_s, tok_s, eidx_s, wsel_s = _finish_mixer(xs, ya_s, yc_s, ys_s, zrest_s, p, l, ms, alpha, n_groups,
                                                    per_group, w_router, b_router)
        y_assign_s = hierarchical_moe(tok_s, eidx_s, wg_bf[l], wu_bf[l], wd_bf[l], 8)
        xs = ln_combine(x1_s, y_assign_s, wsel_s, ms[5], ln2_g[l][None], ln2_b[l][None], alpha)
        outs['ks'].append(k_s.reshape(nbs, 1, heads, dh))
        outs['vs'].append(v_s.reshape(nbs, 1, heads, dh))
        outs['fs'].append(logf_s[:, :heads].reshape(nbs, 1, heads))
        outs['cs'].append(jnp.stack([state_conv[l, :, 1], v_row], axis=1))
        outs['srs'].append(_ssm_untile(h_s[:, :, :ns], nbs, g, pst))
        outs['sis'].append(_ssm_untile(h_s[:, :, ns:], nbs, g, pst))

    st = lambda k: jnp.stack(outs[k])
    return (xp, xs.reshape(nbs, 1, d), st('kp'), st('vp'), st('fp'), st('ks'), st('vs'), st('fs'),
            st('cp'), st('cs'), st('srp'), st('sip'), st('srs'), st('sis'))
```

```python
import functools
import math

import jax
import jax.numpy as jnp
from jax import lax
from jax.experimental import pallas as pl
from jax.experimental.pallas import tpu as pltpu

F32 = jnp.float32
BF16 = jnp.bfloat16
I32 = jnp.int32

LANES = 128
VMEM_LIMIT = 56 * 1024 * 1024
LN_EPS = 1e-5
SSM_CHUNK = 8
MOE_ROWS = 128
NT_DIMS = (((1,), (1,)), ((), ()))


def _cparams(*sem):
    return pltpu.CompilerParams(dimension_semantics=sem, vmem_limit_bytes=VMEM_LIMIT)


def _pick(n, pref):
    if n <= pref:
        return n
    b = pref
    while n % b:
        b //= 2
    return b


def _ada_kernel(c_ref, w_ref, b_ref, o_ref):
    c = c_ref[...]
    s = (c * jax.nn.sigmoid(c)).astype(BF16)
    o_ref[...] = jnp.dot(s, w_ref[...].astype(BF16), preferred_element_type=F32) + b_ref[...]


def ada_modulation(c_all, w_ada, b_ada):
    depth, d, n6 = w_ada.shape
    r = c_all.shape[0]
    bn = _pick(n6, 512)
    return pl.pallas_call(
        _ada_kernel,
        grid=(depth, n6 // bn),
        in_specs=[
            pl.BlockSpec((r, d), lambda l, j: (0, 0)),
            pl.BlockSpec((None, d, bn), lambda l, j: (l, 0, j)),
            pl.BlockSpec((None, 1, bn), lambda l, j: (l, 0, j)),
        ],
        out_specs=pl.BlockSpec((None, r, bn), lambda l, j: (l, 0, j)),
        out_shape=jax.ShapeDtypeStruct((depth, r, n6), F32),
        compiler_params=_cparams("arbitrary", "arbitrary"),
        name="ada_modulation",
    )(c_all, w_ada, b_ada.reshape(depth, 1, n6))


def _mod_spec(mod, bt):
    if mod.shape[1] == 1:
        return pl.BlockSpec((1, 1, mod.shape[2]), lambda b, t: (b, 0, 0))
    return pl.BlockSpec((1, bt, mod.shape[2]), lambda b, t: (b, t, 0))


def _modulate_kernel(x_ref, sc_ref, sh_ref, o_ref):
    o_ref[...] = (x_ref[...] * (1.0 + sc_ref[...]) + sh_ref[...]).astype(o_ref.dtype)


def modulate(x3, sc, sh):
    nb, t, d = x3.shape
    bt = _pick(t, 512)
    return pl.pallas_call(
        _modulate_kernel,
        grid=(nb, t // bt),
        in_specs=[pl.BlockSpec((1, bt, d), lambda b, i: (b, i, 0)), _mod_spec(sc, bt), _mod_spec(sh, bt)],
        out_specs=pl.BlockSpec((1, bt, d), lambda b, i: (b, i, 0)),
        out_shape=jax.ShapeDtypeStruct((nb, t, d), BF16),
        compiler_params=_cparams("arbitrary", "arbitrary"),
        name="modulate",
    )(x3, sc, sh)


def _mm_kernel(a_ref, w_ref, o_ref, wbf_ref):
    @pl.when(pl.program_id(1) == 0)
    def _():
        wbf_ref[...] = w_ref[...].astype(BF16)

    o_ref[...] = jnp.dot(a_ref[...].astype(BF16), wbf_ref[...], preferred_element_type=F32).astype(o_ref.dtype)


def _mm_bf16_kernel(a_ref, w_ref, o_ref):
    o_ref[...] = jnp.dot(a_ref[...].astype(BF16), w_ref[...], preferred_element_type=F32).astype(o_ref.dtype)


def matmul(a, w3, layer, *, out_dtype=F32, bm=1024, bn=512):
    m, k = a.shape
    ncols = w3.shape[2]
    bm = _pick(m, bm)
    bn = _pick(ncols, bn)
    assert ncols % bn == 0 and m % bm == 0
    cast = w3.dtype != BF16
    return pl.pallas_call(
        _mm_kernel if cast else _mm_bf16_kernel,
        grid=(ncols // bn, m // bm),
        in_specs=[
            pl.BlockSpec((bm, k), lambda j, i: (i, 0)),
            pl.BlockSpec((None, k, bn), lambda j, i: (layer, 0, j)),
        ],
        out_specs=pl.BlockSpec((bm, bn), lambda j, i: (i, j)),
        out_shape=jax.ShapeDtypeStruct((m, ncols), out_dtype),
        scratch_shapes=[pltpu.VMEM((k, bn), BF16)] if cast else [],
        compiler_params=_cparams("arbitrary", "arbitrary"),
        name="matmul",
    )(a, w3)


def _forget_kernel(a_ref, w_ref, b_ref, lf_ref, fc_ref, carry_ref, *, cumsum):
    t = pl.program_id(1)
    z = jnp.dot(a_ref[...].astype(BF16), w_ref[...].astype(BF16), preferred_element_type=F32) + b_ref[...]
    lf = jnp.minimum(z, 0.0) - jnp.log1p(jnp.exp(-jnp.abs(z)))
    lf_ref[...] = lf
    if not cumsum:
        fc_ref[...] = lf
        return

    @pl.when(t == 0)
    def _():
        carry_ref[...] = jnp.zeros_like(carry_ref)

    bt = lf.shape[0]
    row = lax.broadcasted_iota(I32, lf.shape, 0)
    acc = lf
    s = 1
    while s < bt:
        acc = acc + jnp.where(row >= s, pltpu.roll(acc, s, axis=0), 0.0)
        s *= 2
    acc = acc + carry_ref[...]
    fc_ref[...] = acc
    carry_ref[...] = acc[bt - 1:bt, :]


def forget_gate(xin, w_f, layer, b_pad, nb, t, cumsum):
    n, d = xin.shape
    bt = _pick(t, 512)
    nt = t // bt
    kern = functools.partial(_forget_kernel, cumsum=cumsum)
    return pl.pallas_call(
        kern,
        grid=(nb, nt),
        in_specs=[
            pl.BlockSpec((bt, d), lambda b, i: (b * nt + i, 0)),
            pl.BlockSpec((None, d, LANES), lambda b, i: (layer, 0, 0)),
            pl.BlockSpec((1, LANES), lambda b, i: (0, 0)),
        ],
        out_specs=[pl.BlockSpec((bt, LANES), lambda b, i: (b * nt + i, 0))] * 2,
        out_shape=[jax.ShapeDtypeStruct((n, LANES), F32)] * 2,
        scratch_shapes=[pltpu.VMEM((1, LANES), F32)],
        compiler_params=_cparams("arbitrary", "arbitrary"),
        name="forget_gate",
    )(xin, w_f, b_pad)


def _flash_kernel(qi_ref, ki_ref, q_ref, k_ref, v_ref, fq_ref, fk_ref, o_ref, m_ref, l_ref, acc_ref, fqc_ref,
                  *, hp, dh):
    hb = pl.program_id(1) * hp
    step = pl.program_id(2)
    qi = qi_ref[step]
    ki = ki_ref[step]

    @pl.when(ki == 0)
    def _():
        m_ref[...] = jnp.full_like(m_ref, -jnp.inf)
        l_ref[...] = jnp.zeros_like(l_ref)
        acc_ref[...] = jnp.zeros_like(acc_ref)
        lane = lax.broadcasted_iota(I32, fq_ref.shape, 1)
        for hh in range(hp):
            fqc_ref[hh] = jnp.sum(jnp.where(lane == hb + hh, fq_ref[...], 0.0), axis=1, keepdims=True)

    def block(masked):
        for hh in range(hp):
            cols = slice(hh * dh, (hh + 1) * dh)
            s = lax.dot_general(q_ref[:, cols].astype(BF16), k_ref[:, cols].astype(BF16), NT_DIMS,
                                preferred_element_type=F32) * (dh ** -0.5)
            s = s + fqc_ref[hh] - fk_ref[pl.ds(hb + hh, 1), :]
            if masked:
                row = lax.broadcasted_iota(I32, s.shape, 0)
                col = lax.broadcasted_iota(I32, s.shape, 1)
                s = jnp.where(col <= row, s, -jnp.inf)
            m_prev = m_ref[hh]
            m_new = jnp.maximum(m_prev, jnp.max(s, axis=1, keepdims=True))
            alpha = jnp.exp(m_prev - m_new)
            p = jnp.exp(s - m_new)
            l_ref[hh] = alpha * l_ref[hh] + jnp.sum(p, axis=1, keepdims=True)
            acc_ref[hh] = alpha * acc_ref[hh] + jnp.dot(p.astype(BF16), v_ref[:, cols].astype(BF16),
                                                        preferred_element_type=F32)
            m_ref[hh] = m_new

    @pl.when(ki < qi)
    def _():
        block(False)

    @pl.when(ki == qi)
    def _():
        block(True)
        for hh in range(hp):
            o_ref[:, hh * dh:(hh + 1) * dh] = (acc_ref[hh] / l_ref[hh]).astype(o_ref.dtype)


def flash_attention(zqkv, fcum, fcum_t, nb, t, heads, dh):
    n = nb * t
    bq = _pick(t, 512)
    nq = t // bq
    hp = 2 if heads % 2 == 0 else 1
    ng = heads // hp
    pairs = [(i, j) for i in range(nq) for j in range(i + 1)]
    qi_tab = jnp.asarray([p[0] for p in pairs], I32)
    ki_tab = jnp.asarray([p[1] for p in pairs], I32)
    kern = functools.partial(_flash_kernel, hp=hp, dh=dh)
    w = hp * dh
    return pl.pallas_call(
        kern,
        grid_spec=pltpu.PrefetchScalarGridSpec(
            num_scalar_prefetch=2,
            grid=(nb, ng, len(pairs)),
            in_specs=[
                pl.BlockSpec((bq, w), lambda b, g, s, qt, kt: (b * nq + qt[s], g)),
                pl.BlockSpec((bq, w), lambda b, g, s, qt, kt: (b * nq + kt[s], ng + g)),
                pl.BlockSpec((bq, w), lambda b, g, s, qt, kt: (b * nq + kt[s], 2 * ng + g)),
                pl.BlockSpec((bq, LANES), lambda b, g, s, qt, kt: (b * nq + qt[s], 0)),
                pl.BlockSpec((None, heads, bq), lambda b, g, s, qt, kt: (b, 0, kt[s])),
            ],
            out_specs=pl.BlockSpec((bq, w), lambda b, g, s, qt, kt: (b * nq + qt[s], g)),
            scratch_shapes=[pltpu.VMEM((hp, bq, 1), F32), pltpu.VMEM((hp, bq, 1), F32),
                            pltpu.VMEM((hp, bq, dh), F32), pltpu.VMEM((hp, bq, 1), F32)],
        ),
        out_shape=jax.ShapeDtypeStruct((n, heads * dh), BF16),
        compiler_params=_cparams("arbitrary", "arbitrary", "arbitrary"),
        name="flash_attention",
    )(qi_tab, ki_tab, zqkv, zqkv, zqkv, fcum, fcum_t)


def _decode_kernel(pt_ref, q_ref, kn_ref, vn_ref, ln_ref, lp_hbm, k_hbm, v_hbm, o_ref,
                   lp_buf, k_buf, v_buf, sem, m_ref, l_ref, acc_ref, car_ref,
                   *, layer, scale, n_seq, n_pages, group):
    b = pl.program_id(0)
    i = pl.program_id(1)
    n_steps = n_pages // group
    step = b * n_steps + i
    slot = lax.rem(step, 2)

    def page_copies(seq, st, sl):
        out = []
        for g in range(group):
            pg = pt_ref[seq * n_pages + (n_pages - 1 - (st * group + g))]
            out.append(pltpu.make_async_copy(lp_hbm.at[layer, pg], lp_buf.at[sl, g], sem.at[sl]))
            out.append(pltpu.make_async_copy(k_hbm.at[layer, pg], k_buf.at[sl, g], sem.at[sl]))
            out.append(pltpu.make_async_copy(v_hbm.at[layer, pg], v_buf.at[sl, g], sem.at[sl]))
        return out

    @pl.when(step == 0)
    def _():
        for cp in page_copies(b, i, slot):
            cp.start()

    @pl.when(step + 1 < n_seq * n_steps)
    def _():
        wrap = i + 1 == n_steps
        for cp in page_copies(jnp.where(wrap, b + 1, b), jnp.where(wrap, 0, i + 1), 1 - slot):
            cp.start()

    for cp in page_copies(b, i, slot):
        cp.wait()
    lp_refs = [lp_buf.at[slot, g] for g in range(group)]
    k_refs = [k_buf.at[slot, g] for g in range(group)]
    v_refs = [v_buf.at[slot, g] for g in range(group)]

    @pl.when(i == 0)
    def _():
        m_ref[...] = jnp.full_like(m_ref, -jnp.inf)
        l_ref[...] = jnp.zeros_like(l_ref)
        acc_ref[...] = jnp.zeros_like(acc_ref)
        car_ref[...] = ln_ref[...]

    q = q_ref[...]
    qs = q * scale
    page = lp_refs[0].shape[0]
    r = lax.broadcasted_iota(I32, (page, page), 0)
    c = lax.broadcasted_iota(I32, (page, page), 1)
    after = jnp.where(c > r, 1.0, 0.0).astype(F32)
    carry = car_ref[...]
    scores = []
    for g in range(group):
        lp = lp_refs[g][...]
        suffix = jnp.dot(after, lp, preferred_element_type=F32, precision=lax.Precision.HIGHEST) + carry
        carry = suffix[0:1, :] + lp[0:1, :]
        scores.append(jnp.sum(k_refs[g][...] * qs[None], axis=-1, keepdims=True) + suffix[:, :, None])
    car_ref[...] = carry
    m_prev = m_ref[...]
    m_new = m_prev
    for s in scores:
        m_new = jnp.maximum(m_new, jnp.max(s, axis=0))
    alpha = jnp.exp(m_prev - m_new)
    l_new = alpha * l_ref[...]
    acc = alpha * acc_ref[...]
    for g in range(group):
        p = jnp.exp(scores[g] - m_new[None])
        l_new = l_new + jnp.sum(p, axis=0)
        acc = acc + jnp.sum(p * v_refs[g][...], axis=0)
    l_ref[...] = l_new
    acc_ref[...] = acc
    m_ref[...] = m_new

    @pl.when(i == n_steps - 1)
    def _():
        s_new = jnp.sum(qs * kn_ref[...], axis=-1, keepdims=True)
        m_fin = jnp.maximum(m_new, s_new)
        a = jnp.exp(m_new - m_fin)
        p_new = jnp.exp(s_new - m_fin)
        o_ref[...] = (a * acc + p_new * vn_ref[...]) / (a * l_new + p_new)


def decode_attention(q, k_new, v_new, logf_new, cache_k, cache_v, cache_logf, page_table, layer):
    nb, heads, dh = q.shape
    page = cache_k.shape[2]
    n_pages = page_table.shape[1]
    group = _pick(n_pages, 4)
    n_steps = n_pages // group
    pt = page_table.reshape(-1).astype(I32)
    kern = functools.partial(_decode_kernel, layer=layer, scale=dh ** -0.5, n_seq=nb, n_pages=n_pages, group=group)
    tok = pl.BlockSpec((None, heads, dh), lambda b, i, pt_ref: (b, 0, 0))
    hbm = pl.BlockSpec(memory_space=pl.ANY)
    return pl.pallas_call(
        kern,
        grid_spec=pltpu.PrefetchScalarGridSpec(
            num_scalar_prefetch=1,
            grid=(nb, n_steps),
            in_specs=[tok, tok, tok, pl.BlockSpec((None, 1, heads), lambda b, i, pt_ref: (b, 0, 0)), hbm, hbm, hbm],
            out_specs=tok,
            scratch_shapes=[pltpu.VMEM((2, group, page, heads), F32),
                            pltpu.VMEM((2, group, page, heads, dh), F32),
                            pltpu.VMEM((2, group, page, heads, dh), F32),
                            pltpu.SemaphoreType.DMA((2,)),
                            pltpu.VMEM((heads, 1), F32), pltpu.VMEM((heads, 1), F32), pltpu.VMEM((heads, dh), F32),
                            pltpu.VMEM((1, heads), F32)],
        ),
        out_shape=jax.ShapeDtypeStruct((nb, heads, dh), F32),
        compiler_params=_cparams("arbitrary", "arbitrary"),
        name="decode_attention",
    )(pt, q, k_new, v_new, logf_new, cache_logf, cache_k, cache_v)


def _conv_kernel(b_ref, c_ref, x_ref, w_ref, prev_ref, y_ref, st_ref, car_ref):
    t = pl.program_id(2)
    v = c_ref[...] * x_ref[...]
    bt = v.shape[0]

    @pl.when(t == 0)
    def _():
        car_ref[...] = prev_ref[0]

    p0 = car_ref[0:1, :]
    p1 = car_ref[1:2, :]
    row = lax.broadcasted_iota(I32, v.shape, 0)
    r1 = jnp.where(row == 0, p1, pltpu.roll(v, 1, axis=0))
    r2 = jnp.where(row == 0, p0, jnp.where(row == 1, p1, pltpu.roll(v, 2, axis=0)))
    w = w_ref[...]
    y = w[0:1, :] * r2 + w[1:2, :] * r1 + w[2:3, :] * v
    y_ref[...] = (b_ref[...] * y).astype(y_ref.dtype)
    tail = v[bt - 2:bt, :]
    car_ref[...] = tail
    st_ref[0] = tail


def short_conv_prompt(zrest, col_b, conv_w_l, prev, nb, t, cw):
    n = nb * t
    bt = _pick(t, 512)
    bc = _pick(cw, 512)
    nt = t // bt
    o = col_b // bc
    per = cw // bc
    zspec = lambda k: pl.BlockSpec((bt, bc), lambda b, c, i: (b * nt + i, o + k * per + c))
    return pl.pallas_call(
        _conv_kernel,
        grid=(nb, per, nt),
        in_specs=[zspec(0), zspec(1), zspec(2),
                  pl.BlockSpec((3, bc), lambda b, c, i: (0, c)),
                  pl.BlockSpec((1, 2, bc), lambda b, c, i: (b, 0, c))],
        out_specs=[pl.BlockSpec((bt, bc), lambda b, c, i: (b * nt + i, c)),
                   pl.BlockSpec((1, 2, bc), lambda b, c, i: (b, 0, c))],
        out_shape=[jax.ShapeDtypeStruct((n, cw), BF16), jax.ShapeDtypeStruct((nb, 2, cw), F32)],
        scratch_shapes=[pltpu.VMEM((2, bc), F32)],
        compiler_params=_cparams("arbitrary", "arbitrary", "arbitrary"),
        name="short_conv_prompt",
    )(zrest, zrest, zrest, conv_w_l, prev)


def _conv_step_kernel(b_ref, c_ref, x_ref, w_ref, p0_ref, p1_ref, y_ref, v_ref):
    v = c_ref[...] * x_ref[...]
    w = w_ref[...]
    y = w[0:1, :] * p0_ref[...] + w[1:2, :] * p1_ref[...] + w[2:3, :] * v
    y_ref[...] = (b_ref[...] * y).astype(y_ref.dtype)
    v_ref[...] = v


def short_conv_step(zrest, col_b, conv_w_l, prev0, prev1, cw):
    nb = zrest.shape[0]
    bc = _pick(cw, 512)
    o = col_b // bc
    per = cw // bc
    zspec = lambda k: pl.BlockSpec((nb, bc), lambda c: (0, o + k * per + c))
    vec = pl.BlockSpec((nb, bc), lambda c: (0, c))
    return pl.pallas_call(
        _conv_step_kernel,
        grid=(per,),
        in_specs=[zspec(0), zspec(1), zspec(2), pl.BlockSpec((3, bc), lambda c: (0, c)), vec, vec],
        out_specs=[vec, vec],
        out_shape=[jax.ShapeDtypeStruct((nb, cw), BF16), jax.ShapeDtypeStruct((nb, cw), F32)],
        compiler_params=_cparams("arbitrary"),
        name="short_conv_step",
    )(zrest, zrest, zrest, conv_w_l, prev0, prev1)


def _ssm_weights_kernel(lre_ref, lim_ref, ldt_ref, bre_ref, bim_ref, cre_ref, cim_ref,
                        t_ref, s_ref, o_ref, al_ref, a1_ref, *, chunk):
    ns = lre_ref.shape[1]
    lr = lre_ref[...]
    li = lim_ref[...]
    dt = jnp.exp(ldt_ref[...])
    mag = jnp.exp(lr * dt)
    ang = li * dt
    ar = mag * jnp.cos(ang)
    ai = mag * jnp.sin(ang)
    den = lr * lr + li * li
    nr = ar - 1.0
    fr = (nr * lr + ai * li) / den
    fi = (ai * lr - nr * li) / den
    bre = bre_ref[...]
    bim = bim_ref[...]
    bbr = fr * bre - fi * bim
    bbi = fr * bim + fi * bre
    cre = cre_ref[...]
    cim = cim_ref[...]
    cfull = jnp.concatenate([cre, -cim], axis=1)
    pows = []
    pr = jnp.ones_like(ar)
    pi = jnp.zeros_like(ar)
    for _ in range(chunk + 1):
        pows.append((pr, pi))
        pr, pi = pr * ar - pi * ai, pr * ai + pi * ar
    a1_ref[...] = jnp.concatenate([ar, ai], axis=1)
    al_ref[...] = jnp.concatenate(list(pows[chunk]), axis=1)
    taps = [None] * chunk
    for j in range(chunk):
        qr, qi = pows[chunk - 1 - j]
        blk = jnp.concatenate([qr * bbr - qi * bbi, qr * bbi + qi * bbr], axis=1)
        s_ref[j * LANES:(j + 1) * LANES, :] = blk.astype(s_ref.dtype)
        taps[chunk - 1 - j] = lax.dot_general(blk, cfull, NT_DIMS, preferred_element_type=F32,
                                              precision=lax.Precision.HIGHEST)
    zero = jnp.zeros((LANES, LANES), F32)
    for j in range(chunk):
        for i in range(chunk):
            t_ref[j * LANES:(j + 1) * LANES, i * LANES:(i + 1) * LANES] = (
                taps[i - j] if i >= j else zero).astype(t_ref.dtype)
    for i in range(chunk):
        qr, qi = pows[i + 1]
        o_ref[i * LANES:(i + 1) * LANES, :] = jnp.concatenate(
            [cre * qr - cim * qi, -cre * qi - cim * qr], axis=1).astype(o_ref.dtype)


def ssm_weights(lam_re, lam_im, log_dt, b_re, b_im, c_re, c_im, chunk):
    g, p = lam_re.shape
    gc = b_re.shape[-1]
    gpt = LANES // gc
    nt = g // gpt
    ns = gpt * p
    eye = jnp.eye(gpt, dtype=F32)
    tile = lambda x: x.reshape(nt, 1, ns)
    ldt = tile(jnp.broadcast_to(log_dt[:, None], (g, p)))
    bdiag = lambda b: jnp.einsum('tgpc,gh->thcgp', b.reshape(nt, gpt, p, gc), eye).reshape(nt, LANES, ns)
    cdiag = lambda c: jnp.einsum('tgcp,gh->thcgp', c.reshape(nt, gpt, gc, p), eye).reshape(nt, LANES, ns)
    row = pl.BlockSpec((None, 1, ns), lambda j: (j, 0, 0))
    mat = pl.BlockSpec((None, LANES, ns), lambda j: (j, 0, 0))
    lc = chunk * LANES
    kern = functools.partial(_ssm_weights_kernel, chunk=chunk)
    return pl.pallas_call(
        kern,
        grid=(nt,),
        in_specs=[row, row, row, mat, mat, mat, mat],
        out_specs=[pl.BlockSpec((None, lc, lc), lambda j: (j, 0, 0)),
                   pl.BlockSpec((None, lc, 2 * ns), lambda j: (j, 0, 0)),
                   pl.BlockSpec((None, lc, 2 * ns), lambda j: (j, 0, 0)),
                   pl.BlockSpec((None, 1, 2 * ns), lambda j: (j, 0, 0)),
                   pl.BlockSpec((None, 1, 2 * ns), lambda j: (j, 0, 0))],
        out_shape=[jax.ShapeDtypeStruct((nt, lc, lc), BF16),
                   jax.ShapeDtypeStruct((nt, lc, 2 * ns), BF16),
                   jax.ShapeDtypeStruct((nt, lc, 2 * ns), BF16),
                   jax.ShapeDtypeStruct((nt, 1, 2 * ns), F32),
                   jax.ShapeDtypeStruct((nt, 1, 2 * ns), F32)],
        compiler_params=_cparams("arbitrary"),
        name="ssm_weights",
    )(tile(lam_re), tile(lam_im), ldt, bdiag(b_re), bdiag(b_im), cdiag(c_re), cdiag(c_im))


def _ssm_kernel(u_ref, h0_ref, t_ref, s_ref, o_ref, al_ref, d_ref, z_ref, hf_ref, ucat_ref, sloc_ref, sprev_ref,
                *, chunk):
    nb, t, _ = u_ref.shape
    tc = t // chunk
    ns = al_ref.shape[1] // 2
    for b in range(nb):
        for j in range(chunk):
            ucat_ref[b * tc:(b + 1) * tc, j * LANES:(j + 1) * LANES] = u_ref[b, pl.ds(j, tc, stride=chunk), :]
    nc = ns // LANES
    for b in range(nb):
        rows = slice(b * tc, (b + 1) * tc)
        sl = jnp.dot(ucat_ref[rows, :].astype(BF16), s_ref[...], preferred_element_type=F32)
        for c in range(2 * nc):
            sloc_ref[c, rows, :] = sl[:, c * LANES:(c + 1) * LANES]

    def step(k, h):
        at = pl.ds(k, nb, stride=tc)
        new = [None] * (2 * nc)
        for c in range(nc):
            lanes = slice(c * LANES, (c + 1) * LANES)
            alr = al_ref[:, lanes]
            ali = al_ref[:, ns + c * LANES:ns + (c + 1) * LANES]
            hr = h[c]
            hi = h[nc + c]
            sprev_ref[c, at, :] = hr
            sprev_ref[nc + c, at, :] = hi
            new[c] = alr * hr - ali * hi + sloc_ref[c, at, :]
            new[nc + c] = alr * hi + ali * hr + sloc_ref[nc + c, at, :]
        return tuple(new)

    h0 = h0_ref[...]
    hfin = lax.fori_loop(0, tc, step, tuple(h0[:, c * LANES:(c + 1) * LANES] for c in range(2 * nc)))
    hf_ref[...] = jnp.concatenate(hfin, axis=1)
    for b in range(nb):
        rows = slice(b * tc, (b + 1) * tc)
        uc = ucat_ref[rows, :]
        sp = jnp.concatenate([sprev_ref[c, rows, :] for c in range(2 * nc)], axis=1)
        y = jnp.dot(uc.astype(BF16), t_ref[...], preferred_element_type=F32)
        y = y + lax.dot_general(sp.astype(BF16), o_ref[...], NT_DIMS, preferred_element_type=F32)
        z = jax.nn.gelu(y + d_ref[...] * uc)
        for i in range(chunk):
            z_ref[b, pl.ds(i, tc, stride=chunk), :] = z[:, i * LANES:(i + 1) * LANES]


def ssm_prompt(zrest3, col_u, wts, h0cat, d_cat, chunk):
    tmat, smat, omat, al, _ = wts
    nb, t, _ = zrest3.shape
    nt, lc, ns2 = smat.shape
    tc = t // chunk
    kern = functools.partial(_ssm_kernel, chunk=chunk)
    return pl.pallas_call(
        kern,
        grid=(nt,),
        in_specs=[pl.BlockSpec((nb, t, LANES), lambda j: (0, 0, col_u // LANES + j)),
                  pl.BlockSpec((None, nb, ns2), lambda j: (j, 0, 0)),
                  pl.BlockSpec((None, lc, lc), lambda j: (j, 0, 0)),
                  pl.BlockSpec((None, lc, ns2), lambda j: (j, 0, 0)),
                  pl.BlockSpec((None, lc, ns2), lambda j: (j, 0, 0)),
                  pl.BlockSpec((None, 1, ns2), lambda j: (j, 0, 0)),
                  pl.BlockSpec((None, 1, lc), lambda j: (j, 0, 0))],
        out_specs=[pl.BlockSpec((nb, t, LANES), lambda j: (0, 0, j)),
                   pl.BlockSpec((None, nb, ns2), lambda j: (j, 0, 0))],
        out_shape=[jax.ShapeDtypeStruct((nb, t, nt * LANES), F32), jax.ShapeDtypeStruct((nt, nb, ns2), F32)],
        scratch_shapes=[pltpu.VMEM((nb * tc, lc), F32), pltpu.VMEM((ns2 // LANES, nb * tc, LANES), F32),
                        pltpu.VMEM((ns2 // LANES, nb * tc, LANES), F32)],
        compiler_params=_cparams("arbitrary"),
        name="ssm_prompt",
    )(zrest3, h0cat, tmat, smat, omat, al, d_cat)


def _ssm_step_kernel(u_ref, h0_ref, s_ref, o_ref, a1_ref, d_ref, z_ref, h_ref, *, chunk):
    ns = a1_ref.shape[1] // 2
    u = u_ref[...]
    x = jnp.dot(u.astype(BF16), s_ref[(chunk - 1) * LANES:chunk * LANES, :], preferred_element_type=F32)
    ar = a1_ref[:, :ns]
    ai = a1_ref[:, ns:]
    h0 = h0_ref[...]
    hr = ar * h0[:, :ns] - ai * h0[:, ns:] + x[:, :ns]
    hi = ar * h0[:, ns:] + ai * h0[:, :ns] + x[:, ns:]
    h = jnp.concatenate([hr, hi], axis=1)
    h_ref[...] = h
    y = lax.dot_general(h.astype(BF16), o_ref[...], NT_DIMS, preferred_element_type=F32)
    z_ref[...] = jax.nn.gelu(y + d_ref[...] * u)


def ssm_step(zrest, col_u, wts, cmat, h0cat, d_t, chunk):
    _, smat, _, _, a1 = wts
    nb = zrest.shape[0]
    nt, lc, ns2 = smat.shape
    kern = functools.partial(_ssm_step_kernel, chunk=chunk)
    return pl.pallas_call(
        kern,
        grid=(nt,),
        in_specs=[pl.BlockSpec((nb, LANES), lambda j: (0, col_u // LANES + j)),
                  pl.BlockSpec((None, nb, ns2), lambda j: (j, 0, 0)),
                  pl.BlockSpec((None, lc, ns2), lambda j: (j, 0, 0)),
                  pl.BlockSpec((None, LANES, ns2), lambda j: (j, 0, 0)),
                  pl.BlockSpec((None, 1, ns2), lambda j: (j, 0, 0)),
                  pl.BlockSpec((None, 1, LANES), lambda j: (j, 0, 0))],
        out_specs=[pl.BlockSpec((nb, LANES), lambda j: (0, j)),
                   pl.BlockSpec((None, nb, ns2), lambda j: (j, 0, 0))],
        out_shape=[jax.ShapeDtypeStruct((nb, nt * LANES), F32), jax.ShapeDtypeStruct((nt, nb, ns2), F32)],
        compiler_params=_cparams("arbitrary"),
        name="ssm_step",
    )(zrest, h0cat, smat, cmat, a1, d_t)


def _glu_kernel(a_ref, zt_ref, w_ref, o_ref, wbf_ref):
    @pl.when(pl.program_id(1) == 0)
    def _():
        wbf_ref[...] = w_ref[...].astype(BF16)

    acc = jnp.dot(a_ref[...].astype(BF16), wbf_ref[...], preferred_element_type=F32)
    o_ref[...] = (zt_ref[...] * jax.nn.sigmoid(acc)).astype(o_ref.dtype)


def glu(z, w3, layer):
    m, k = z.shape
    bm = _pick(m, 1024)
    bn = _pick(k, 512)
    return pl.pallas_call(
        _glu_kernel,
        grid=(k // bn, m // bm),
        in_specs=[pl.BlockSpec((bm, k), lambda j, i: (i, 0)),
                  pl.BlockSpec((bm, bn), lambda j, i: (i, j)),
                  pl.BlockSpec((None, k, bn), lambda j, i: (layer, 0, j))],
        out_specs=pl.BlockSpec((bm, bn), lambda j, i: (i, j)),
        out_shape=jax.ShapeDtypeStruct((m, k), BF16),
        scratch_shapes=[pltpu.VMEM((k, bn), BF16)],
        compiler_params=_cparams("arbitrary", "arbitrary"),
        name="glu",
    )(z, z, w3)


def _merge_kernel(ya_ref, yc_ref, ys_ref, ga_ref, gc_ref, gs_ref, wa_ref, wc_ref, ws_ref, o_ref,
                  wa_bf, wc_bf, ws_bf):
    @pl.when(pl.program_id(1) == 0)
    def _():
        wa_bf[...] = wa_ref[...].astype(BF16)
        wc_bf[...] = wc_ref[...].astype(BF16)
        ws_bf[...] = ws_ref[...].astype(BF16)

    acc = jax.nn.sigmoid(ga_ref[...]) * jnp.dot(ya_ref[...], wa_bf[...], preferred_element_type=F32)
    acc = acc + jax.nn.sigmoid(gc_ref[...]) * jnp.dot(yc_ref[...], wc_bf[...], preferred_element_type=F32)
    acc = acc + jax.nn.sigmoid(gs_ref[...]) * jnp.dot(ys_ref[...], ws_bf[...], preferred_element_type=F32)
    o_ref[...] = acc.astype(o_ref.dtype)


def merge_branches(ya, yc, ys, zrest, col_g, w_a, w_c, w_s, layer, d):
    m = ya.shape[0]
    bm = _pick(m, 512)
    bn = _pick(d, 512)
    og = col_g // bn
    per = d // bn
    yspec = lambda y: pl.BlockSpec((bm, y.shape[1]), lambda j, i: (i, 0))
    gspec = lambda k: pl.BlockSpec((bm, bn), lambda j, i: (i, og + k * per + j))
    wspec = lambda w: pl.BlockSpec((None, w.shape[1], bn), lambda j, i: (layer, 0, j))
    return pl.pallas_call(
        _merge_kernel,
        grid=(per, m // bm),
        in_specs=[yspec(ya), yspec(yc), yspec(ys), gspec(0), gspec(1), gspec(2), wspec(w_a), wspec(w_c), wspec(w_s)],
        out_specs=pl.BlockSpec((bm, bn), lambda j, i: (i, j)),
        out_shape=jax.ShapeDtypeStruct((m, d), BF16),
        scratch_shapes=[pltpu.VMEM((w_a.shape[1], bn), BF16), pltpu.VMEM((w_c.shape[1], bn), BF16),
                        pltpu.VMEM((w_s.shape[1], bn), BF16)],
        compiler_params=_cparams("arbitrary", "arbitrary"),
        name="merge_branches",
    )(ya, yc, ys, zrest, zrest, zrest, w_a, w_c, w_s)


def _gather_lane_tiles(ref, lead=()):
    return jnp.concatenate([ref[lead + (slice(None), c, slice(None))] for c in range(ref.shape[-2])], axis=1)


def _scatter_lane_tiles(ref, val, lead=()):
    for c in range(ref.shape[-2]):
        ref[lead + (slice(None), c, slice(None))] = val[:, c * LANES:(c + 1) * LANES]


def _layer_norm(v, g, b):
    mu = jnp.mean(v, axis=-1, keepdims=True)
    var = jnp.mean(jnp.square(v - mu), axis=-1, keepdims=True)
    return (v - mu) * lax.rsqrt(var + LN_EPS) * g + b


def _ln_route_kernel(x_ref, y_ref, gate_ref, g_ref, b_ref, sc_ref, sh_ref, wr_ref, br_ref,
                     x1_ref, tok_ref, eidx_ref, wts_ref, *, alpha, n_groups, per_group):
    x1 = _layer_norm(alpha * x_ref[0] + (1.0 + gate_ref[0]) * y_ref[0], g_ref[...], b_ref[...])
    x1_ref[0] = x1
    tok = x1 * (1.0 + sc_ref[0]) + sh_ref[0]
    _scatter_lane_tiles(tok_ref, tok, (0,))
    logit = jnp.dot(tok.astype(BF16), wr_ref[...].astype(BF16), preferred_element_type=F32) + br_ref[...]
    lane_i = lax.broadcasted_iota(I32, logit.shape, 1)
    lane = lane_i.astype(F32)
    big = float(LANES)
    neg = -jnp.inf
    gl = jnp.where(lane < n_groups, logit, neg)
    gmax = jnp.max(gl, axis=1, keepdims=True)
    gidx = jnp.min(jnp.where(gl == gmax, lane, big), axis=1, keepdims=True)
    gprob = 1.0 / jnp.sum(jnp.exp(gl - gmax), axis=1, keepdims=True)
    lo = n_groups + gidx * per_group
    el = jnp.where((lane >= lo) & (lane < lo + per_group), logit, neg)
    t1 = jnp.max(el, axis=1, keepdims=True)
    i1 = jnp.min(jnp.where(el == t1, lane, big), axis=1, keepdims=True)
    el2 = jnp.where(lane == i1, neg, el)
    t2 = jnp.max(el2, axis=1, keepdims=True)
    i2 = jnp.min(jnp.where(el2 == t2, lane, big), axis=1, keepdims=True)
    e2 = jnp.exp(t2 - t1)
    w1 = gprob / (1.0 + e2)
    w2 = gprob * e2 / (1.0 + e2)
    eidx_ref[0] = jnp.where(lane_i == 0, i1 - n_groups, jnp.where(lane_i == 1, i2 - n_groups, 0.0)).astype(I32)
    wts_ref[0] = jnp.where(lane_i == 0, w1, jnp.where(lane_i == 1, w2, 0.0))


def ln_route(x3, y3, gate, ln_g, ln_b, sc, sh, w_router, b_router, alpha, n_groups, per_group):
    nb, t, d = x3.shape
    bt = _pick(t, 256)
    blk = pl.BlockSpec((1, bt, d), lambda b, i: (b, i, 0))
    row = pl.BlockSpec((1, d), lambda b, i: (0, 0))
    sel = pl.BlockSpec((1, bt, LANES), lambda b, i: (b, i, 0))
    tiles = pl.BlockSpec((1, bt, d // LANES, LANES), lambda b, i: (b, i, 0, 0))
    kern = functools.partial(_ln_route_kernel, alpha=alpha, n_groups=n_groups, per_group=per_group)
    return pl.pallas_call(
        kern,
        grid=(nb, t // bt),
        in_specs=[blk, blk, _mod_spec(gate, bt), row, row, _mod_spec(sc, bt), _mod_spec(sh, bt),
                  pl.BlockSpec((d, LANES), lambda b, i: (0, 0)), pl.BlockSpec((1, LANES), lambda b, i: (0, 0))],
        out_specs=[blk, tiles, sel, sel],
        out_shape=[jax.ShapeDtypeStruct((nb, t, d), F32), jax.ShapeDtypeStruct((nb, t, d // LANES, LANES), F32),
                   jax.ShapeDtypeStruct((nb, t, LANES), I32), jax.ShapeDtypeStruct((nb, t, LANES), F32)],
        compiler_params=_cparams("arbitrary", "arbitrary"),
        name="ln_route",
    )(x3, y3, gate, ln_g, ln_b, sc, sh, w_router, b_router)


def _ln_combine_kernel(x_ref, y0_ref, y1_ref, wts_ref, gate_ref, g_ref, b_ref, o_ref, *, alpha):
    w = wts_ref[0]
    ffn = w[:, 0:1] * _gather_lane_tiles(y0_ref, (0,)) + w[:, 1:2] * _gather_lane_tiles(y1_ref, (0,))
    o_ref[0] = _layer_norm(alpha * x_ref[0] + (1.0 + gate_ref[0]) * ffn, g_ref[...], b_ref[...])


def ln_combine(x3, y_assign, wts, gate, ln_g, ln_b, alpha):
    nb, t, d = x3.shape
    bt = _pick(t, 256)
    blk = pl.BlockSpec((1, bt, d), lambda b, i: (b, i, 0))
    row = pl.BlockSpec((1, d), lambda b, i: (0, 0))
    ysp = lambda k: pl.BlockSpec((None, 1, bt, d // LANES, LANES), lambda b, i: (k, b, i, 0, 0))
    kern = functools.partial(_ln_combine_kernel, alpha=alpha)
    return pl.pallas_call(
        kern,
        grid=(nb, t // bt),
        in_specs=[blk, ysp(0), ysp(1), pl.BlockSpec((1, bt, LANES), lambda b, i: (b, i, 0)),
                  _mod_spec(gate, bt), row, row],
        out_specs=blk,
        out_shape=jax.ShapeDtypeStruct((nb, t, d), F32),
        compiler_params=_cparams("arbitrary", "arbitrary"),
        name="ln_combine",
    )(x3, y_assign, y_assign, wts, gate, ln_g, ln_b)


def _rank_kernel(e_ref, rank_ref, cnt_ref, car_ref):
    first = (pl.program_id(0) == 0) & (pl.program_id(1) == 0)

    @pl.when(first)
    def _():
        car_ref[...] = jnp.zeros_like(car_ref)

    e = e_ref[...]
    rb = e.shape[1]
    ex = lax.broadcasted_iota(I32, (LANES, rb), 0)
    onehot = jnp.where(ex == e, 1.0, 0.0).astype(BF16)
    r = lax.broadcasted_iota(I32, (rb, rb), 0)
    c = lax.broadcasted_iota(I32, (rb, rb), 1)
    upto = jnp.where(r <= c, 1.0, 0.0).astype(BF16)
    cum = jnp.dot(onehot, upto, preferred_element_type=F32)
    oh = onehot.astype(F32)
    rank = jnp.sum(oh * (cum - 1.0 + car_ref[...]), axis=0, keepdims=True)
    rank_ref[...] = rank.astype(I32)
    car_ref[...] = car_ref[...] + jnp.sum(oh, axis=1, keepdims=True)
    cnt_ref[...] = jnp.broadcast_to(car_ref[...], cnt_ref.shape).astype(I32)


def expert_ranks(e_rows):
    two, nblk, _, rb = e_rows.shape
    return pl.pallas_call(
        _rank_kernel,
        grid=(two, nblk),
        in_specs=[pl.BlockSpec((None, None, 1, rb), lambda k, i: (k, i, 0, 0))],
        out_specs=[pl.BlockSpec((None, None, 1, rb), lambda k, i: (k, i, 0, 0)),
                   pl.BlockSpec((LANES, LANES), lambda k, i: (0, 0))],
        out_shape=[jax.ShapeDtypeStruct(e_rows.shape, I32), jax.ShapeDtypeStruct((LANES, LANES), I32)],
        scratch_shapes=[pltpu.VMEM((LANES, 1), F32)],
        compiler_params=_cparams("arbitrary", "arbitrary"),
        name="expert_ranks",
    )(e_rows)


def _move_rows_kernel(idx_ref, src_ref, *rest, per_step, n_src, scatter):
    dst_ref, sem = rest[-2:]
    base = pl.program_id(0) * per_step

    def copy(u):
        a = base + u
        if scatter:
            return pltpu.make_async_copy(src_ref.at[pl.ds(lax.rem(a, n_src), 1)], dst_ref.at[pl.ds(idx_ref[a], 1)],
                                         sem)
        return pltpu.make_async_copy(src_ref.at[pl.ds(idx_ref[a], 1)], dst_ref.at[pl.ds(a, 1)], sem)

    def start(u, c):
        copy(u).start()
        return c

    def wait(u, c):
        copy(u).wait()
        return c

    lax.fori_loop(0, per_step, start, 0)
    lax.fori_loop(0, per_step, wait, 0)


def move_rows(idx, src, n_dst, scatter):
    na = idx.shape[0]
    n_src = src.shape[0]
    row = src.shape[1:]
    per_step = _pick(na, 256)
    kern = functools.partial(_move_rows_kernel, per_step=per_step, n_src=n_src, scatter=scatter)
    hbm = pl.BlockSpec(memory_space=pl.ANY)
    operands = (idx, src, jnp.zeros((n_dst,) + row, src.dtype)) if scatter else (idx, src)
    return pl.pallas_call(
        kern,
        grid_spec=pltpu.PrefetchScalarGridSpec(
            num_scalar_prefetch=1,
            grid=(na // per_step,),
            in_specs=[hbm] * (len(operands) - 1),
            out_specs=hbm,
            scratch_shapes=[pltpu.SemaphoreType.DMA(())],
        ),
        out_shape=jax.ShapeDtypeStruct((n_dst,) + row, src.dtype),
        input_output_aliases={2: 0} if scatter else {},
        compiler_params=_cparams("arbitrary"),
        name="move_rows_scatter" if scatter else "move_rows_gather",
    )(*operands)


def _expert_kernel(be_ref, nu_ref, x_ref, wg_ref, wu_ref, wd_ref, o_ref):
    del be_ref

    @pl.when(pl.program_id(0) < nu_ref[0])
    def _():
        xb = _gather_lane_tiles(x_ref).astype(BF16)
        g = jnp.dot(xb, wg_ref[...], preferred_element_type=F32)
        u = jnp.dot(xb, wu_ref[...], preferred_element_type=F32)
        h = (g * jax.nn.sigmoid(g) * u).astype(BF16)
        _scatter_lane_tiles(o_ref, jnp.dot(h, wd_ref[...], preferred_element_type=F32))

    @pl.when(pl.program_id(0) >= nu_ref[0])
    def _():
        o_ref[...] = jnp.zeros_like(o_ref)


def expert_blocks(x_sorted, blk_e, n_used, wg, wu, wd, rows):
    r, nc, _ = x_sorted.shape
    d = nc * LANES
    hid = wg.shape[-1]
    nblk = r // rows
    live = lambda i, nu: jnp.minimum(i, nu[0] - 1)
    return pl.pallas_call(
        _expert_kernel,
        grid_spec=pltpu.PrefetchScalarGridSpec(
            num_scalar_prefetch=2,
            grid=(nblk,),
            in_specs=[pl.BlockSpec((rows, nc, LANES), lambda i, be, nu: (live(i, nu), 0, 0)),
                      pl.BlockSpec((None, d, hid), lambda i, be, nu: (be[live(i, nu)], 0, 0)),
                      pl.BlockSpec((None, d, hid), lambda i, be, nu: (be[live(i, nu)], 0, 0)),
                      pl.BlockSpec((None, hid, d), lambda i, be, nu: (be[live(i, nu)], 0, 0))],
            out_specs=pl.BlockSpec((rows, nc, LANES), lambda i, be, nu: (i, 0, 0)),
        ),
        out_shape=jax.ShapeDtypeStruct((r, nc, LANES), F32),
        compiler_params=_cparams("arbitrary"),
        name="expert_blocks",
    )(blk_e, n_used, x_sorted, wg, wu, wd)


def hierarchical_moe(tok3, eidx, wg, wu, wd, rows):
    nb, t, nc, _ = tok3.shape
    n = nb * t
    n_exp = wg.shape[0]
    e2 = eidx.reshape(n, LANES)[:, :2].T
    if n >= LANES:
        rb = _pick(n, 512)
        rank, cnt = expert_ranks(e2.reshape(2, n // rb, 1, rb))
        rank = rank.reshape(-1)
        counts = cnt[:n_exp, 0]
    else:
        ef = e2.reshape(-1)
        ar = jnp.arange(2 * n)
        rank = jnp.sum((ef[:, None] == ef[None, :]) & (ar[None, :] < ar[:, None]), axis=1).astype(I32)
        counts = jnp.sum(ef[:, None] == jnp.arange(n_exp)[None, :], axis=0).astype(I32)
    ef = e2.reshape(-1)
    nblk = min(-(-(2 * n + n_exp * (rows - 1)) // rows), 2 * n)
    padded = (counts + rows - 1) // rows * rows
    pad_end = jnp.cumsum(padded)
    dest = ((pad_end - padded)[ef] + rank).astype(I32)
    blk_e = jnp.minimum(jnp.sum(pad_end[None, :] <= (jnp.arange(nblk) * rows)[:, None], axis=1), n_exp - 1).astype(I32)
    n_used = (pad_end[-1:] // rows).astype(I32)
    x_sorted = move_rows(dest, tok3.reshape(n, nc, LANES), nblk * rows, scatter=True)
    y_rows = expert_blocks(x_sorted, blk_e, n_used, wg, wu, wd, rows)
    y_assign = move_rows(dest, y_rows, 2 * n, scatter=False)
    return y_assign.reshape(2, nb, t, nc, LANES)


def _dims(p):
    d = p['w_o'].shape[-1]
    aw = p['w_br_attn'].shape[1]
    cw = p['w_br_conv'].shape[1]
    sw = p['w_br_ssm'].shape[1]
    return d, aw, cw, sw


def _mixer_common(xin, p, layer, nb, t, heads, cumsum):
    zqkv = matmul(xin, p['w_qkv'], layer, bn=1024)
    zrest = matmul(xin, p['w_rest'], layer, bn=1024)
    b_pad = jnp.zeros((1, LANES), F32).at[0, :heads].set(p['b_forget'][layer])
    logf, fcum = forget_gate(xin, p['w_f'], layer, b_pad, nb, t, cumsum)
    return zqkv, zrest, logf, fcum


def _finish_mixer(x3, ya, yc, ys, zrest, p, layer, mods, alpha, n_groups, per_group, w_router, b_router):
    d, aw, cw, sw = _dims(p)
    nb, t, _ = x3.shape
    merged = merge_branches(ya, yc, ys, zrest, 3 * cw + sw, p['w_br_attn'], p['w_br_conv'], p['w_br_ssm'], layer, d)
    mix = matmul(merged, p['w_o'], layer)
    sh1, sc1, g1, sh2, sc2, g2 = mods
    return ln_route(x3, mix.reshape(nb, t, d), g1, p['ln1_g'][layer][None], p['ln1_b'][layer][None], sc2, sh2,
                    w_router, b_router, alpha, n_groups, per_group)


def _ssm_tiles(state, nt):
    nb = state.shape[0]
    return state.reshape(nb, nt, -1).transpose(1, 0, 2)


def _ssm_untile(h, nb, g, p):
    return h.transpose(1, 0, 2).reshape(nb, g, p)


def kernel(x_prompt, x_sample, c_prompt, c_sample, cache_k, cache_v, cache_logf, page_table, state_conv, state_ssm_re, state_ssm_im, w_ada, b_ada, ln1_g, ln1_b, ln2_g, ln2_b, w_in, b_forget, conv_w, ssm_lambda_re, ssm_lambda_im, ssm_log_dt, ssm_b_re, ssm_b_im, ssm_c_re, ssm_c_im, ssm_d, ssm_w_glu, w_br_attn, w_br_conv, w_br_ssm, w_o, router_w_group, router_b_group, router_w_expert, router_b_expert, moe_w_gate, moe_w_up, moe_w_down):
    p = dict(b_forget=b_forget, w_br_attn=w_br_attn, w_br_conv=w_br_conv, w_br_ssm=w_br_ssm, w_o=w_o,
             ln1_g=ln1_g, ln1_b=ln1_b, ln2_g=ln2_g, ln2_b=ln2_b)
    depth = w_in.shape[0]
    nbp, t, d = x_prompt.shape
    nbs, ts, _ = x_sample.shape
    assert ts == 1
    heads, dh = cache_k.shape[3], cache_k.shape[4]
    aw = heads * dh
    cw = conv_w.shape[-1]
    sw = ssm_d.shape[-1]
    g, pst = ssm_lambda_re.shape[1], ssm_lambda_re.shape[2]
    gc = ssm_b_re.shape[-1]
    nt = g * gc // LANES
    ns = (LANES // gc) * pst
    n_groups = router_w_group.shape[-1]
    n_exp = router_w_expert.shape[-1]
    per_group = n_exp // n_groups
    alpha = (2 * depth) ** 0.25
    chunk = SSM_CHUNK
    col_conv = 0
    col_u = 3 * cw

    r = nbp + nbs
    rpad = -(-r // 8) * 8
    c_all = jnp.zeros((rpad, d), F32).at[:nbp].set(c_prompt).at[nbp:r].set(c_sample)
    mod_all = ada_modulation(c_all, w_ada, b_ada)

    p['w_qkv'] = w_in[:, :, :3 * aw].astype(BF16)
    p['w_rest'] = w_in[:, :, 3 * aw + heads:].astype(BF16)
    p['w_f'] = jnp.pad(w_in[:, :, 3 * aw:3 * aw + heads], ((0, 0), (0, 0), (0, LANES - heads)))

    wg_bf = moe_w_gate.astype(BF16)
    wu_bf = moe_w_up.astype(BF16)
    wd_bf = moe_w_down.astype(BF16)

    xp = x_prompt
    xs = x_sample.reshape(1, nbs, d)
    outs = {k: [] for k in ('kp', 'vp', 'fp', 'ks', 'vs', 'fs', 'cp', 'cs', 'srp', 'sip', 'srs', 'sis')}
    zero_conv = jnp.zeros((nbp, 2, cw), F32)
    zero_h = jnp.zeros((nt, nbp, 2 * ns), F32)
    for l in range(depth):
        mp = [mod_all[l, :nbp, i * d:(i + 1) * d].reshape(nbp, 1, d) for i in range(6)]
        ms = [mod_all[l, nbp:r, i * d:(i + 1) * d].reshape(1, nbs, d) for i in range(6)]
        w_router = jnp.zeros((d, LANES), F32).at[:, :n_groups].set(router_w_group[l]) \
            .at[:, n_groups:n_groups + n_exp].set(router_w_expert[l])
        b_router = jnp.zeros((1, LANES), F32).at[0, :n_groups].set(router_b_group[l]) \
            .at[0, n_groups:n_groups + n_exp].set(router_b_expert[l])
        wts = ssm_weights(ssm_lambda_re[l], ssm_lambda_im[l], ssm_log_dt[l], ssm_b_re[l], ssm_b_im[l],
                          ssm_c_re[l], ssm_c_im[l], chunk)
        d_t = ssm_d[l].reshape(nt, 1, LANES)
        d_cat = jnp.tile(d_t, (1, 1, chunk))
        eye = jnp.eye(LANES // gc, dtype=F32)
        cdiag = lambda c: jnp.einsum('tgcp,gh->thcgp', c.reshape(nt, LANES // gc, gc, pst), eye).reshape(nt, LANES, ns)
        cmat = jnp.concatenate([cdiag(ssm_c_re[l]), -cdiag(ssm_c_im[l])], axis=2).astype(BF16)

        n = nbp * t
        xin = modulate(xp, mp[1], mp[0]).reshape(n, d)
        zqkv, zrest, logf, fcum = _mixer_common(xin, p, l, nbp, t, heads, True)
        fcum_t = fcum[:, :heads].reshape(nbp, t, heads).transpose(0, 2, 1)
        ya = flash_attention(zqkv, fcum, fcum_t, nbp, t, heads, dh)
        yc, conv_p = short_conv_prompt(zrest, col_conv, conv_w[l], zero_conv, nbp, t, cw)
        zs, hfin = ssm_prompt(zrest.reshape(nbp, t, -1), col_u, wts, zero_h, d_cat, chunk)
        ys = glu(zs.reshape(n, sw), ssm_w_glu, l)
        x1, tok, eidx, wsel = _finish_mixer(xp, ya, yc, ys, zrest, p, l, mp, alpha, n_groups, per_group,
                                            w_router, b_router)
        y_assign = hierarchical_moe(tok, eidx, wg_bf[l], wu_bf[l], wd_bf[l], MOE_ROWS)
        xp = ln_combine(x1, y_assign, wsel, mp[5], ln2_g[l][None], ln2_b[l][None], alpha)
        outs['kp'].append(zqkv[:, aw:2 * aw].reshape(nbp, t, heads, dh))
        outs['vp'].append(zqkv[:, 2 * aw:].reshape(nbp, t, heads, dh))
        outs['fp'].append(logf[:, :heads].reshape(nbp, t, heads))
        outs['cp'].append(conv_p)
        outs['srp'].append(_ssm_untile(hfin[:, :, :ns], nbp, g, pst))
        outs['sip'].append(_ssm_untile(hfin[:, :, ns:], nbp, g, pst))

        xin_s = modulate(xs, ms[1], ms[0]).reshape(nbs, d)
        zqkv_s, zrest_s, logf_s, _ = _mixer_common(xin_s, p, l, 1, nbs, heads, False)
        q_s = zqkv_s[:, :aw].reshape(nbs, heads, dh)
        k_s = zqkv_s[:, aw:2 * aw].reshape(nbs, heads, dh)
        v_s = zqkv_s[:, 2 * aw:].reshape(nbs, heads, dh)
        ya_s = decode_attention(q_s, k_s, v_s, logf_s[:, :heads].reshape(nbs, 1, heads), cache_k, cache_v, cache_logf,
                                page_table, l)
        ya_s = ya_s.reshape(nbs, aw).astype(BF16)
        yc_s, v_row = short_conv_step(zrest_s, col_conv, conv_w[l], state_conv[l, :, 0], state_conv[l, :, 1], cw)
        h0 = jnp.concatenate([_ssm_tiles(state_ssm_re[l], nt), _ssm_tiles(state_ssm_im[l], nt)], axis=2)
        zs_s, h_s = ssm_step(zrest_s, col_u, wts, cmat, h0, d_t, chunk)
        ys_s = glu(zs_s, ssm_w_glu, l)
        x1_s, tok_s, eidx_s, wsel_s = _finish_mixer(xs, ya_s, yc_s, ys_s, zrest_s, p, l, ms, alpha, n_groups,
                                                    per_group, w_router, b_router)
        y_assign_s = hierarchical_moe(tok_s, eidx_s, wg_bf[l], wu_bf[l], wd_bf[l], 8)
        xs = ln_combine(x1_s, y_assign_s, wsel_s, ms[5], ln2_g[l][None], ln2_b[l][None], alpha)
        outs['ks'].append(k_s.reshape(nbs, 1, heads, dh))
        outs['vs'].append(v_s.reshape(nbs, 1, heads, dh))
        outs['fs'].append(logf_s[:, :heads].reshape(nbs, 1, heads))
        outs['cs'].append(jnp.stack([state_conv[l, :, 1], v_row], axis=1))
        outs['srs'].append(_ssm_untile(h_s[:, :, :ns], nbs, g, pst))
        outs['sis'].append(_ssm_untile(h_s[:, :, ns:], nbs, g, pst))

    st = lambda k: jnp.stack(outs[k])
    return (xp, xs.reshape(nbs, 1, d), st('kp'), st('vp'), st('fp'), st('ks'), st('vs'), st('fs'),
            st('cp'), st('cs'), st('srp'), st('sip'), st('srs'), st('sis'))
```

```python
import functools
import math

import jax
import jax.numpy as jnp
from jax import lax
from jax.experimental import pallas as pl
from jax.experimental.pallas import tpu as pltpu

F32 = jnp.float32
BF16 = jnp.bfloat16
I32 = jnp.int32

LANES = 128
VMEM_LIMIT = 56 * 1024 * 1024
LN_EPS = 1e-5
SSM_CHUNK = 8
MOE_ROWS = 128
NT_DIMS = (((1,), (1,)), ((), ()))


def _cparams(*sem):
    return pltpu.CompilerParams(dimension_semantics=sem, vmem_limit_bytes=VMEM_LIMIT)


def _pick(n, pref):
    if n <= pref:
        return n
    b = pref
    while n % b:
        b //= 2
    return b


def _ada_kernel(c_ref, w_ref, b_ref, o_ref):
    c = c_ref[...]
    s = (c * jax.nn.sigmoid(c)).astype(BF16)
    o_ref[...] = jnp.dot(s, w_ref[...].astype(BF16), preferred_element_type=F32) + b_ref[...]


def ada_modulation(c_all, w_ada, b_ada):
    depth, d, n6 = w_ada.shape
    r = c_all.shape[0]
    bn = _pick(n6, 512)
    return pl.pallas_call(
        _ada_kernel,
        grid=(depth, n6 // bn),
        in_specs=[
            pl.BlockSpec((r, d), lambda l, j: (0, 0)),
            pl.BlockSpec((None, d, bn), lambda l, j: (l, 0, j)),
            pl.BlockSpec((None, 1, bn), lambda l, j: (l, 0, j)),
        ],
        out_specs=pl.BlockSpec((None, r, bn), lambda l, j: (l, 0, j)),
        out_shape=jax.ShapeDtypeStruct((depth, r, n6), F32),
        compiler_params=_cparams("arbitrary", "arbitrary"),
        name="ada_modulation",
    )(c_all, w_ada, b_ada.reshape(depth, 1, n6))


def _mod_spec(mod, bt):
    if mod.shape[1] == 1:
        return pl.BlockSpec((1, 1, mod.shape[2]), lambda b, t: (b, 0, 0))
    return pl.BlockSpec((1, bt, mod.shape[2]), lambda b, t: (b, t, 0))


def _modulate_kernel(x_ref, sc_ref, sh_ref, o_ref):
    o_ref[...] = (x_ref[...] * (1.0 + sc_ref[...]) + sh_ref[...]).astype(o_ref.dtype)


def modulate(x3, sc, sh):
    nb, t, d = x3.shape
    bt = _pick(t, 512)
    return pl.pallas_call(
        _modulate_kernel,
        grid=(nb, t // bt),
        in_specs=[pl.BlockSpec((1, bt, d), lambda b, i: (b, i, 0)), _mod_spec(sc, bt), _mod_spec(sh, bt)],
        out_specs=pl.BlockSpec((1, bt, d), lambda b, i: (b, i, 0)),
        out_shape=jax.ShapeDtypeStruct((nb, t, d), BF16),
        compiler_params=_cparams("arbitrary", "arbitrary"),
        name="modulate",
    )(x3, sc, sh)


def _mm_kernel(a_ref, w_ref, o_ref, wbf_ref):
    @pl.when(pl.program_id(1) == 0)
    def _():
        wbf_ref[...] = w_ref[...].astype(BF16)

    o_ref[...] = jnp.dot(a_ref[...].astype(BF16), wbf_ref[...], preferred_element_type=F32).astype(o_ref.dtype)


def matmul(a, w3, layer, *, out_dtype=F32, bm=1024, bn=512):
    m, k = a.shape
    ncols = w3.shape[2]
    bm = _pick(m, bm)
    bn = _pick(ncols, bn)
    assert ncols % bn == 0 and m % bm == 0
    return pl.pallas_call(
        _mm_kernel,
        grid=(ncols // bn, m // bm),
        in_specs=[
            pl.BlockSpec((bm, k), lambda j, i: (i, 0)),
            pl.BlockSpec((None, k, bn), lambda j, i: (layer, 0, j)),
        ],
        out_specs=pl.BlockSpec((bm, bn), lambda j, i: (i, j)),
        out_shape=jax.ShapeDtypeStruct((m, ncols), out_dtype),
        scratch_shapes=[pltpu.VMEM((k, bn), BF16)],
        compiler_params=_cparams("arbitrary", "arbitrary"),
        name="matmul",
    )(a, w3)


def _mm_nt_kernel(a_ref, w_ref, o_ref, wbf_ref):
    @pl.when(pl.program_id(1) == 0)
    def _():
        wbf_ref[...] = w_ref[...].astype(BF16)

    o_ref[...] = lax.dot_general(a_ref[...].astype(BF16), wbf_ref[...], NT_DIMS,
                                 preferred_element_type=F32).astype(o_ref.dtype)


def _weight_rows(row0, bn, k):
    return pl.BlockSpec((pl.Element(bn), pl.Element(k)), lambda j, i: (pl.multiple_of(row0 + j * bn, 8), 0))


def matmul_nt(a, wt, row0, ncols, *, out_dtype=F32, bm=1024, bn=512):
    m, k = a.shape
    bm = _pick(m, bm)
    bn = _pick(ncols, bn)
    assert ncols % bn == 0 and m % bm == 0 and row0 % 8 == 0 and row0 + ncols <= wt.shape[0]
    return pl.pallas_call(
        _mm_nt_kernel,
        grid=(ncols // bn, m // bm),
        in_specs=[pl.BlockSpec((bm, k), lambda j, i: (i, 0)), _weight_rows(row0, bn, k)],
        out_specs=pl.BlockSpec((bm, bn), lambda j, i: (i, j)),
        out_shape=jax.ShapeDtypeStruct((m, ncols), out_dtype),
        scratch_shapes=[pltpu.VMEM((bn, k), BF16)],
        compiler_params=_cparams("arbitrary", "arbitrary"),
        name="matmul_nt",
    )(a, wt)


def _forget_kernel(a_ref, w_ref, b_ref, lf_ref, fc_ref, carry_ref, *, cumsum):
    t = pl.program_id(1)
    z = lax.dot_general(a_ref[...].astype(BF16), w_ref[...].astype(BF16), NT_DIMS,
                        preferred_element_type=F32) + b_ref[...]
    lf = jnp.minimum(z, 0.0) - jnp.log1p(jnp.exp(-jnp.abs(z)))
    lf_ref[...] = lf
    if not cumsum:
        fc_ref[...] = lf
        return

    @pl.when(t == 0)
    def _():
        carry_ref[...] = jnp.zeros_like(carry_ref)

    bt = lf.shape[0]
    row = lax.broadcasted_iota(I32, lf.shape, 0)
    acc = lf
    s = 1
    while s < bt:
        acc = acc + jnp.where(row >= s, pltpu.roll(acc, s, axis=0), 0.0)
        s *= 2
    acc = acc + carry_ref[...]
    fc_ref[...] = acc
    carry_ref[...] = acc[bt - 1:bt, :]


def forget_gate(xin, wt, row0, b_pad, nb, t, cumsum):
    n, d = xin.shape
    bt = _pick(t, 512)
    nt = t // bt
    assert row0 % 8 == 0 and row0 + LANES <= wt.shape[0]
    kern = functools.partial(_forget_kernel, cumsum=cumsum)
    return pl.pallas_call(
        kern,
        grid=(nb, nt),
        in_specs=[
            pl.BlockSpec((bt, d), lambda b, i: (b * nt + i, 0)),
            pl.BlockSpec((pl.Element(LANES), pl.Element(d)), lambda b, i: (row0, 0)),
            pl.BlockSpec((1, LANES), lambda b, i: (0, 0)),
        ],
        out_specs=[pl.BlockSpec((bt, LANES), lambda b, i: (b * nt + i, 0))] * 2,
        out_shape=[jax.ShapeDtypeStruct((n, LANES), F32)] * 2,
        scratch_shapes=[pltpu.VMEM((1, LANES), F32)],
        compiler_params=_cparams("arbitrary", "arbitrary"),
        name="forget_gate",
    )(xin, wt, b_pad)


def _flash_kernel(qi_ref, ki_ref, q_ref, k_ref, v_ref, fq_ref, fk_ref, o_ref, m_ref, l_ref, acc_ref, fqc_ref,
                  *, hp, dh):
    hb = pl.program_id(1) * hp
    step = pl.program_id(2)
    qi = qi_ref[step]
    ki = ki_ref[step]

    @pl.when(ki == 0)
    def _():
        m_ref[...] = jnp.full_like(m_ref, -jnp.inf)
        l_ref[...] = jnp.zeros_like(l_ref)
        acc_ref[...] = jnp.zeros_like(acc_ref)
        lane = lax.broadcasted_iota(I32, fq_ref.shape, 1)
        for hh in range(hp):
            fqc_ref[hh] = jnp.sum(jnp.where(lane == hb + hh, fq_ref[...], 0.0), axis=1, keepdims=True)

    def block(masked):
        for hh in range(hp):
            cols = slice(hh * dh, (hh + 1) * dh)
            s = lax.dot_general(q_ref[:, cols].astype(BF16), k_ref[:, cols].astype(BF16), NT_DIMS,
                                preferred_element_type=F32) * (dh ** -0.5)
            s = s + fqc_ref[hh] - fk_ref[pl.ds(hb + hh, 1), :]
            if masked:
                row = lax.broadcasted_iota(I32, s.shape, 0)
                col = lax.broadcasted_iota(I32, s.shape, 1)
                s = jnp.where(col <= row, s, -jnp.inf)
            m_prev = m_ref[hh]
            m_new = jnp.maximum(m_prev, jnp.max(s, axis=1, keepdims=True))
            alpha = jnp.exp(m_prev - m_new)
            p = jnp.exp(s - m_new)
            l_ref[hh] = alpha * l_ref[hh] + jnp.sum(p, axis=1, keepdims=True)
            acc_ref[hh] = alpha * acc_ref[hh] + jnp.dot(p.astype(BF16), v_ref[:, cols].astype(BF16),
                                                        preferred_element_type=F32)
            m_ref[hh] = m_new

    @pl.when(ki < qi)
    def _():
        block(False)

    @pl.when(ki == qi)
    def _():
        block(True)
        for hh in range(hp):
            o_ref[:, hh * dh:(hh + 1) * dh] = (acc_ref[hh] / l_ref[hh]).astype(o_ref.dtype)


def flash_attention(zqkv, fcum, fcum_t, nb, t, heads, dh):
    n = nb * t
    bq = _pick(t, 512)
    nq = t // bq
    hp = 2 if heads % 2 == 0 else 1
    ng = heads // hp
    pairs = [(i, j) for i in range(nq) for j in range(i + 1)]
    qi_tab = jnp.asarray([p[0] for p in pairs], I32)
    ki_tab = jnp.asarray([p[1] for p in pairs], I32)
    kern = functools.partial(_flash_kernel, hp=hp, dh=dh)
    w = hp * dh
    return pl.pallas_call(
        kern,
        grid_spec=pltpu.PrefetchScalarGridSpec(
            num_scalar_prefetch=2,
            grid=(nb, ng, len(pairs)),
            in_specs=[
                pl.BlockSpec((bq, w), lambda b, g, s, qt, kt: (b * nq + qt[s], g)),
                pl.BlockSpec((bq, w), lambda b, g, s, qt, kt: (b * nq + kt[s], ng + g)),
                pl.BlockSpec((bq, w), lambda b, g, s, qt, kt: (b * nq + kt[s], 2 * ng + g)),
                pl.BlockSpec((bq, LANES), lambda b, g, s, qt, kt: (b * nq + qt[s], 0)),
                pl.BlockSpec((None, heads, bq), lambda b, g, s, qt, kt: (b, 0, kt[s])),
            ],
            out_specs=pl.BlockSpec((bq, w), lambda b, g, s, qt, kt: (b * nq + qt[s], g)),
            scratch_shapes=[pltpu.VMEM((hp, bq, 1), F32), pltpu.VMEM((hp, bq, 1), F32),
                            pltpu.VMEM((hp, bq, dh), F32), pltpu.VMEM((hp, bq, 1), F32)],
        ),
        out_shape=jax.ShapeDtypeStruct((n, heads * dh), BF16),
        compiler_params=_cparams("arbitrary", "arbitrary", "arbitrary"),
        name="flash_attention",
    )(qi_tab, ki_tab, zqkv, zqkv, zqkv, fcum, fcum_t)


def _decode_kernel(pt_ref, q_ref, kn_ref, vn_ref, ln_ref, lp_hbm, k_hbm, v_hbm, o_ref,
                   lp_buf, k_buf, v_buf, sem, m_ref, l_ref, acc_ref, car_ref,
                   *, layer, scale, n_seq, n_pages, group):
    b = pl.program_id(0)
    i = pl.program_id(1)
    n_steps = n_pages // group
    step = b * n_steps + i
    slot = lax.rem(step, 2)

    def page_copies(seq, st, sl):
        out = []
        for g in range(group):
            pg = pt_ref[seq * n_pages + (n_pages - 1 - (st * group + g))]
            out.append(pltpu.make_async_copy(lp_hbm.at[layer, pg], lp_buf.at[sl, g], sem.at[sl]))
            out.append(pltpu.make_async_copy(k_hbm.at[layer, pg], k_buf.at[sl, g], sem.at[sl]))
            out.append(pltpu.make_async_copy(v_hbm.at[layer, pg], v_buf.at[sl, g], sem.at[sl]))
        return out

    @pl.when(step == 0)
    def _():
        for cp in page_copies(b, i, slot):
            cp.start()

    @pl.when(step + 1 < n_seq * n_steps)
    def _():
        wrap = i + 1 == n_steps
        for cp in page_copies(jnp.where(wrap, b + 1, b), jnp.where(wrap, 0, i + 1), 1 - slot):
            cp.start()

    for cp in page_copies(b, i, slot):
        cp.wait()
    lp_refs = [lp_buf.at[slot, g] for g in range(group)]
    k_refs = [k_buf.at[slot, g] for g in range(group)]
    v_refs = [v_buf.at[slot, g] for g in range(group)]

    @pl.when(i == 0)
    def _():
        m_ref[...] = jnp.full_like(m_ref, -jnp.inf)
        l_ref[...] = jnp.zeros_like(l_ref)
        acc_ref[...] = jnp.zeros_like(acc_ref)
        car_ref[...] = ln_ref[...]

    q = q_ref[...]
    qs = q * scale
    page = lp_refs[0].shape[0]
    r = lax.broadcasted_iota(I32, (page, page), 0)
    c = lax.broadcasted_iota(I32, (page, page), 1)
    after = jnp.where(c > r, 1.0, 0.0).astype(F32)
    carry = car_ref[...]
    scores = []
    for g in range(group):
        lp = lp_refs[g][...]
        suffix = jnp.dot(after, lp, preferred_element_type=F32, precision=lax.Precision.HIGHEST) + carry
        carry = suffix[0:1, :] + lp[0:1, :]
        scores.append(jnp.sum(k_refs[g][...] * qs[None], axis=-1, keepdims=True) + suffix[:, :, None])
    car_ref[...] = carry
    m_prev = m_ref[...]
    m_new = m_prev
    for s in scores:
        m_new = jnp.maximum(m_new, jnp.max(s, axis=0))
    alpha = jnp.exp(m_prev - m_new)
    l_new = alpha * l_ref[...]
    acc = alpha * acc_ref[...]
    for g in range(group):
        p = jnp.exp(scores[g] - m_new[None])
        l_new = l_new + jnp.sum(p, axis=0)
        acc = acc + jnp.sum(p * v_refs[g][...], axis=0)
    l_ref[...] = l_new
    acc_ref[...] = acc
    m_ref[...] = m_new

    @pl.when(i == n_steps - 1)
    def _():
        s_new = jnp.sum(qs * kn_ref[...], axis=-1, keepdims=True)
        m_fin = jnp.maximum(m_new, s_new)
        a = jnp.exp(m_new - m_fin)
        p_new = jnp.exp(s_new - m_fin)
        o_ref[...] = (a * acc + p_new * vn_ref[...]) / (a * l_new + p_new)


def decode_attention(q, k_new, v_new, logf_new, cache_k, cache_v, cache_logf, page_table, layer):
    nb, heads, dh = q.shape
    page = cache_k.shape[2]
    n_pages = page_table.shape[1]
    group = _pick(n_pages, 4)
    n_steps = n_pages // group
    pt = page_table.reshape(-1).astype(I32)
    kern = functools.partial(_decode_kernel, layer=layer, scale=dh ** -0.5, n_seq=nb, n_pages=n_pages, group=group)
    tok = pl.BlockSpec((None, heads, dh), lambda b, i, pt_ref: (b, 0, 0))
    hbm = pl.BlockSpec(memory_space=pl.ANY)
    return pl.pallas_call(
        kern,
        grid_spec=pltpu.PrefetchScalarGridSpec(
            num_scalar_prefetch=1,
            grid=(nb, n_steps),
            in_specs=[tok, tok, tok, pl.BlockSpec((None, 1, heads), lambda b, i, pt_ref: (b, 0, 0)), hbm, hbm, hbm],
            out_specs=tok,
            scratch_shapes=[pltpu.VMEM((2, group, page, heads), F32),
                            pltpu.VMEM((2, group, page, heads, dh), F32),
                            pltpu.VMEM((2, group, page, heads, dh), F32),
                            pltpu.SemaphoreType.DMA((2,)),
                            pltpu.VMEM((heads, 1), F32), pltpu.VMEM((heads, 1), F32), pltpu.VMEM((heads, dh), F32),
                            pltpu.VMEM((1, heads), F32)],
        ),
        out_shape=jax.ShapeDtypeStruct((nb, heads, dh), F32),
        compiler_params=_cparams("arbitrary", "arbitrary"),
        name="decode_attention",
    )(pt, q, k_new, v_new, logf_new, cache_logf, cache_k, cache_v)


def _conv_kernel(b_ref, c_ref, x_ref, w_ref, prev_ref, y_ref, st_ref, car_ref):
    t = pl.program_id(2)
    v = c_ref[...] * x_ref[...]
    bt = v.shape[0]

    @pl.when(t == 0)
    def _():
        car_ref[...] = prev_ref[0]

    p0 = car_ref[0:1, :]
    p1 = car_ref[1:2, :]
    row = lax.broadcasted_iota(I32, v.shape, 0)
    r1 = jnp.where(row == 0, p1, pltpu.roll(v, 1, axis=0))
    r2 = jnp.where(row == 0, p0, jnp.where(row == 1, p1, pltpu.roll(v, 2, axis=0)))
    w = w_ref[...]
    y = w[0:1, :] * r2 + w[1:2, :] * r1 + w[2:3, :] * v
    y_ref[...] = (b_ref[...] * y).astype(y_ref.dtype)
    tail = v[bt - 2:bt, :]
    car_ref[...] = tail
    st_ref[0] = tail


def short_conv_prompt(zrest, col_b, conv_w_l, prev, nb, t, cw):
    n = nb * t
    bt = _pick(t, 512)
    bc = _pick(cw, 512)
    nt = t // bt
    o = col_b // bc
    per = cw // bc
    zspec = lambda k: pl.BlockSpec((bt, bc), lambda b, c, i: (b * nt + i, o + k * per + c))
    return pl.pallas_call(
        _conv_kernel,
        grid=(nb, per, nt),
        in_specs=[zspec(0), zspec(1), zspec(2),
                  pl.BlockSpec((3, bc), lambda b, c, i: (0, c)),
                  pl.BlockSpec((1, 2, bc), lambda b, c, i: (b, 0, c))],
        out_specs=[pl.BlockSpec((bt, bc), lambda b, c, i: (b * nt + i, c)),
                   pl.BlockSpec((1, 2, bc), lambda b, c, i: (b, 0, c))],
        out_shape=[jax.ShapeDtypeStruct((n, cw), BF16), jax.ShapeDtypeStruct((nb, 2, cw), F32)],
        scratch_shapes=[pltpu.VMEM((2, bc), F32)],
        compiler_params=_cparams("arbitrary", "arbitrary", "arbitrary"),
        name="short_conv_prompt",
    )(zrest, zrest, zrest, conv_w_l, prev)


def _conv_step_kernel(b_ref, c_ref, x_ref, w_ref, p0_ref, p1_ref, y_ref, v_ref):
    v = c_ref[...] * x_ref[...]
    w = w_ref[...]
    y = w[0:1, :] * p0_ref[...] + w[1:2, :] * p1_ref[...] + w[2:3, :] * v
    y_ref[...] = (b_ref[...] * y).astype(y_ref.dtype)
    v_ref[...] = v


def short_conv_step(zrest, col_b, conv_w_l, prev0, prev1, cw):
    nb = zrest.shape[0]
    bc = _pick(cw, 512)
    o = col_b // bc
    per = cw // bc
    zspec = lambda k: pl.BlockSpec((nb, bc), lambda c: (0, o + k * per + c))
    vec = pl.BlockSpec((nb, bc), lambda c: (0, c))
    return pl.pallas_call(
        _conv_step_kernel,
        grid=(per,),
        in_specs=[zspec(0), zspec(1), zspec(2), pl.BlockSpec((3, bc), lambda c: (0, c)), vec, vec],
        out_specs=[vec, vec],
        out_shape=[jax.ShapeDtypeStruct((nb, cw), BF16), jax.ShapeDtypeStruct((nb, cw), F32)],
        compiler_params=_cparams("arbitrary"),
        name="short_conv_step",
    )(zrest, zrest, zrest, conv_w_l, prev0, prev1)


def _ssm_weights_kernel(lre_ref, lim_ref, ldt_ref, bre_ref, bim_ref, cre_ref, cim_ref,
                        t_ref, s_ref, o_ref, al_ref, a1_ref, *, chunk):
    ns = lre_ref.shape[1]
    lr = lre_ref[...]
    li = lim_ref[...]
    dt = jnp.exp(ldt_ref[...])
    mag = jnp.exp(lr * dt)
    ang = li * dt
    ar = mag * jnp.cos(ang)
    ai = mag * jnp.sin(ang)
    den = lr * lr + li * li
    nr = ar - 1.0
    fr = (nr * lr + ai * li) / den
    fi = (ai * lr - nr * li) / den
    bre = bre_ref[...]
    bim = bim_ref[...]
    bbr = fr * bre - fi * bim
    bbi = fr * bim + fi * bre
    cre = cre_ref[...]
    cim = cim_ref[...]
    cfull = jnp.concatenate([cre, -cim], axis=1)
    pows = []
    pr = jnp.ones_like(ar)
    pi = jnp.zeros_like(ar)
    for _ in range(chunk + 1):
        pows.append((pr, pi))
        pr, pi = pr * ar - pi * ai, pr * ai + pi * ar
    a1_ref[...] = jnp.concatenate([ar, ai], axis=1)
    al_ref[...] = jnp.concatenate(list(pows[chunk]), axis=1)
    taps = [None] * chunk
    for j in range(chunk):
        qr, qi = pows[chunk - 1 - j]
        blk = jnp.concatenate([qr * bbr - qi * bbi, qr * bbi + qi * bbr], axis=1)
        s_ref[j * LANES:(j + 1) * LANES, :] = blk.astype(s_ref.dtype)
        taps[chunk - 1 - j] = lax.dot_general(blk, cfull, NT_DIMS, preferred_element_type=F32,
                                              precision=lax.Precision.HIGHEST)
    zero = jnp.zeros((LANES, LANES), F32)
    for j in range(chunk):
        for i in range(chunk):
            t_ref[j * LANES:(j + 1) * LANES, i * LANES:(i + 1) * LANES] = (
                taps[i - j] if i >= j else zero).astype(t_ref.dtype)
    for i in range(chunk):
        qr, qi = pows[i + 1]
        o_ref[i * LANES:(i + 1) * LANES, :] = jnp.concatenate(
            [cre * qr - cim * qi, -cre * qi - cim * qr], axis=1).astype(o_ref.dtype)


def ssm_weights(lam_re, lam_im, log_dt, b_re, b_im, c_re, c_im, chunk):
    g, p = lam_re.shape
    gc = b_re.shape[-1]
    gpt = LANES // gc
    nt = g // gpt
    ns = gpt * p
    eye = jnp.eye(gpt, dtype=F32)
    tile = lambda x: x.reshape(nt, 1, ns)
    ldt = tile(jnp.broadcast_to(log_dt[:, None], (g, p)))
    bdiag = lambda b: jnp.einsum('tgpc,gh->thcgp', b.reshape(nt, gpt, p, gc), eye).reshape(nt, LANES, ns)
    cdiag = lambda c: jnp.einsum('tgcp,gh->thcgp', c.reshape(nt, gpt, gc, p), eye).reshape(nt, LANES, ns)
    row = pl.BlockSpec((None, 1, ns), lambda j: (j, 0, 0))
    mat = pl.BlockSpec((None, LANES, ns), lambda j: (j, 0, 0))
    lc = chunk * LANES
    kern = functools.partial(_ssm_weights_kernel, chunk=chunk)
    return pl.pallas_call(
        kern,
        grid=(nt,),
        in_specs=[row, row, row, mat, mat, mat, mat],
        out_specs=[pl.BlockSpec((None, lc, lc), lambda j: (j, 0, 0)),
                   pl.BlockSpec((None, lc, 2 * ns), lambda j: (j, 0, 0)),
                   pl.BlockSpec((None, lc, 2 * ns), lambda j: (j, 0, 0)),
                   pl.BlockSpec((None, 1, 2 * ns), lambda j: (j, 0, 0)),
                   pl.BlockSpec((None, 1, 2 * ns), lambda j: (j, 0, 0))],
        out_shape=[jax.ShapeDtypeStruct((nt, lc, lc), BF16),
                   jax.ShapeDtypeStruct((nt, lc, 2 * ns), BF16),
                   jax.ShapeDtypeStruct((nt, lc, 2 * ns), BF16),
                   jax.ShapeDtypeStruct((nt, 1, 2 * ns), F32),
                   jax.ShapeDtypeStruct((nt, 1, 2 * ns), F32)],
        compiler_params=_cparams("arbitrary"),
        name="ssm_weights",
    )(tile(lam_re), tile(lam_im), ldt, bdiag(b_re), bdiag(b_im), cdiag(c_re), cdiag(c_im))


def _ssm_kernel(u_ref, h0_ref, t_ref, s_ref, o_ref, al_ref, d_ref, z_ref, hf_ref, ucat_ref, sloc_ref, sprev_ref,
                *, chunk):
    nb, t, _ = u_ref.shape
    tc = t // chunk
    ns = al_ref.shape[1] // 2
    for b in range(nb):
        for j in range(chunk):
            ucat_ref[b * tc:(b + 1) * tc, j * LANES:(j + 1) * LANES] = u_ref[b, pl.ds(j, tc, stride=chunk), :]
    nc = ns // LANES
    for b in range(nb):
        rows = slice(b * tc, (b + 1) * tc)
        sl = jnp.dot(ucat_ref[rows, :].astype(BF16), s_ref[...], preferred_element_type=F32)
        for c in range(2 * nc):
            sloc_ref[c, rows, :] = sl[:, c * LANES:(c + 1) * LANES]

    def step(k, h):
        at = pl.ds(k, nb, stride=tc)
        new = [None] * (2 * nc)
        for c in range(nc):
            lanes = slice(c * LANES, (c + 1) * LANES)
            alr = al_ref[:, lanes]
            ali = al_ref[:, ns + c * LANES:ns + (c + 1) * LANES]
            hr = h[c]
            hi = h[nc + c]
            sprev_ref[c, at, :] = hr
            sprev_ref[nc + c, at, :] = hi
            new[c] = alr * hr - ali * hi + sloc_ref[c, at, :]
            new[nc + c] = alr * hi + ali * hr + sloc_ref[nc + c, at, :]
        return tuple(new)

    h0 = h0_ref[...]
    hfin = lax.fori_loop(0, tc, step, tuple(h0[:, c * LANES:(c + 1) * LANES] for c in range(2 * nc)))
    hf_ref[...] = jnp.concatenate(hfin, axis=1)
    for b in range(nb):
        rows = slice(b * tc, (b + 1) * tc)
        uc = ucat_ref[rows, :]
        sp = jnp.concatenate([sprev_ref[c, rows, :] for c in range(2 * nc)], axis=1)
        y = jnp.dot(uc.astype(BF16), t_ref[...], preferred_element_type=F32)
        y = y + lax.dot_general(sp.astype(BF16), o_ref[...], NT_DIMS, preferred_element_type=F32)
        z = jax.nn.gelu(y + d_ref[...] * uc)
        for i in range(chunk):
            z_ref[b, pl.ds(i, tc, stride=chunk), :] = z[:, i * LANES:(i + 1) * LANES]


def ssm_prompt(zrest3, col_u, wts, h0cat, d_cat, chunk):
    tmat, smat, omat, al, _ = wts
    nb, t, _ = zrest3.shape
    nt, lc, ns2 = smat.shape
    tc = t // chunk
    kern = functools.partial(_ssm_kernel, chunk=chunk)
    return pl.pallas_call(
        kern,
        grid=(nt,),
        in_specs=[pl.BlockSpec((nb, t, LANES), lambda j: (0, 0, col_u // LANES + j)),
                  pl.BlockSpec((None, nb, ns2), lambda j: (j, 0, 0)),
                  pl.BlockSpec((None, lc, lc), lambda j: (j, 0, 0)),
                  pl.BlockSpec((None, lc, ns2), lambda j: (j, 0, 0)),
                  pl.BlockSpec((None, lc, ns2), lambda j: (j, 0, 0)),
                  pl.BlockSpec((None, 1, ns2), lambda j: (j, 0, 0)),
                  pl.BlockSpec((None, 1, lc), lambda j: (j, 0, 0))],
        out_specs=[pl.BlockSpec((nb, t, LANES), lambda j: (0, 0, j)),
                   pl.BlockSpec((None, nb, ns2), lambda j: (j, 0, 0))],
        out_shape=[jax.ShapeDtypeStruct((nb, t, nt * LANES), F32), jax.ShapeDtypeStruct((nt, nb, ns2), F32)],
        scratch_shapes=[pltpu.VMEM((nb * tc, lc), F32), pltpu.VMEM((ns2 // LANES, nb * tc, LANES), F32),
                        pltpu.VMEM((ns2 // LANES, nb * tc, LANES), F32)],
        compiler_params=_cparams("arbitrary"),
        name="ssm_prompt",
    )(zrest3, h0cat, tmat, smat, omat, al, d_cat)


def _ssm_step_kernel(u_ref, h0_ref, s_ref, o_ref, a1_ref, d_ref, z_ref, h_ref, *, chunk):
    ns = a1_ref.shape[1] // 2
    u = u_ref[...]
    x = jnp.dot(u.astype(BF16), s_ref[(chunk - 1) * LANES:chunk * LANES, :], preferred_element_type=F32)
    ar = a1_ref[:, :ns]
    ai = a1_ref[:, ns:]
    h0 = h0_ref[...]
    hr = ar * h0[:, :ns] - ai * h0[:, ns:] + x[:, :ns]
    hi = ar * h0[:, ns:] + ai * h0[:, :ns] + x[:, ns:]
    h = jnp.concatenate([hr, hi], axis=1)
    h_ref[...] = h
    y = lax.dot_general(h.astype(BF16), o_ref[...], NT_DIMS, preferred_element_type=F32)
    z_ref[...] = jax.nn.gelu(y + d_ref[...] * u)


def ssm_step(zrest, col_u, wts, cmat, h0cat, d_t, chunk):
    _, smat, _, _, a1 = wts
    nb = zrest.shape[0]
    nt, lc, ns2 = smat.shape
    kern = functools.partial(_ssm_step_kernel, chunk=chunk)
    return pl.pallas_call(
        kern,
        grid=(nt,),
        in_specs=[pl.BlockSpec((nb, LANES), lambda j: (0, col_u // LANES + j)),
                  pl.BlockSpec((None, nb, ns2), lambda j: (j, 0, 0)),
                  pl.BlockSpec((None, lc, ns2), lambda j: (j, 0, 0)),
                  pl.BlockSpec((None, LANES, ns2), lambda j: (j, 0, 0)),
                  pl.BlockSpec((None, 1, ns2), lambda j: (j, 0, 0)),
                  pl.BlockSpec((None, 1, LANES), lambda j: (j, 0, 0))],
        out_specs=[pl.BlockSpec((nb, LANES), lambda j: (0, j)),
                   pl.BlockSpec((None, nb, ns2), lambda j: (j, 0, 0))],
        out_shape=[jax.ShapeDtypeStruct((nb, nt * LANES), F32), jax.ShapeDtypeStruct((nt, nb, ns2), F32)],
        compiler_params=_cparams("arbitrary"),
        name="ssm_step",
    )(zrest, h0cat, smat, cmat, a1, d_t)


def _glu_kernel(a_ref, zt_ref, w_ref, o_ref, wbf_ref):
    @pl.when(pl.program_id(1) == 0)
    def _():
        wbf_ref[...] = w_ref[...].astype(BF16)

    acc = jnp.dot(a_ref[...].astype(BF16), wbf_ref[...], preferred_element_type=F32)
    o_ref[...] = (zt_ref[...] * jax.nn.sigmoid(acc)).astype(o_ref.dtype)


def glu(z, w3, layer):
    m, k = z.shape
    bm = _pick(m, 1024)
    bn = _pick(k, 512)
    return pl.pallas_call(
        _glu_kernel,
        grid=(k // bn, m // bm),
        in_specs=[pl.BlockSpec((bm, k), lambda j, i: (i, 0)),
                  pl.BlockSpec((bm, bn), lambda j, i: (i, j)),
                  pl.BlockSpec((None, k, bn), lambda j, i: (layer, 0, j))],
        out_specs=pl.BlockSpec((bm, bn), lambda j, i: (i, j)),
        out_shape=jax.ShapeDtypeStruct((m, k), BF16),
        scratch_shapes=[pltpu.VMEM((k, bn), BF16)],
        compiler_params=_cparams("arbitrary", "arbitrary"),
        name="glu",
    )(z, z, w3)


def _merge_kernel(ya_ref, yc_ref, ys_ref, ga_ref, gc_ref, gs_ref, wa_ref, wc_ref, ws_ref, o_ref,
                  wa_bf, wc_bf, ws_bf):
    @pl.when(pl.program_id(1) == 0)
    def _():
        wa_bf[...] = wa_ref[...].astype(BF16)
        wc_bf[...] = wc_ref[...].astype(BF16)
        ws_bf[...] = ws_ref[...].astype(BF16)

    acc = jax.nn.sigmoid(ga_ref[...]) * jnp.dot(ya_ref[...], wa_bf[...], preferred_element_type=F32)
    acc = acc + jax.nn.sigmoid(gc_ref[...]) * jnp.dot(yc_ref[...], wc_bf[...], preferred_element_type=F32)
    acc = acc + jax.nn.sigmoid(gs_ref[...]) * jnp.dot(ys_ref[...], ws_bf[...], preferred_element_type=F32)
    o_ref[...] = acc.astype(o_ref.dtype)


def merge_branches(ya, yc, ys, zrest, col_g, w_a, w_c, w_s, layer, d):
    m = ya.shape[0]
    bm = _pick(m, 512)
    bn = _pick(d, 512)
    og = col_g // bn
    per = d // bn
    yspec = lambda y: pl.BlockSpec((bm, y.shape[1]), lambda j, i: (i, 0))
    gspec = lambda k: pl.BlockSpec((bm, bn), lambda j, i: (i, og + k * per + j))
    wspec = lambda w: pl.BlockSpec((None, w.shape[1], bn), lambda j, i: (layer, 0, j))
    return pl.pallas_call(
        _merge_kernel,
        grid=(per, m // bm),
        in_specs=[yspec(ya), yspec(yc), yspec(ys), gspec(0), gspec(1), gspec(2), wspec(w_a), wspec(w_c), wspec(w_s)],
        out_specs=pl.BlockSpec((bm, bn), lambda j, i: (i, j)),
        out_shape=jax.ShapeDtypeStruct((m, d), BF16),
        scratch_shapes=[pltpu.VMEM((w_a.shape[1], bn), BF16), pltpu.VMEM((w_c.shape[1], bn), BF16),
                        pltpu.VMEM((w_s.shape[1], bn), BF16)],
        compiler_params=_cparams("arbitrary", "arbitrary"),
        name="merge_branches",
    )(ya, yc, ys, zrest, zrest, zrest, w_a, w_c, w_s)


def _gather_lane_tiles(ref, lead=()):
    return jnp.concatenate([ref[lead + (slice(None), c, slice(None))] for c in range(ref.shape[-2])], axis=1)


def _scatter_lane_tiles(ref, val, lead=()):
    for c in range(ref.shape[-2]):
        ref[lead + (slice(None), c, slice(None))] = val[:, c * LANES:(c + 1) * LANES]


def _layer_norm(v, g, b):
    mu = jnp.mean(v, axis=-1, keepdims=True)
    var = jnp.mean(jnp.square(v - mu), axis=-1, keepdims=True)
    return (v - mu) * lax.rsqrt(var + LN_EPS) * g + b


def _ln_route_kernel(x_ref, y_ref, gate_ref, g_ref, b_ref, sc_ref, sh_ref, wr_ref, br_ref,
                     x1_ref, tok_ref, eidx_ref, wts_ref, *, alpha, n_groups, per_group):
    x1 = _layer_norm(alpha * x_ref[0] + (1.0 + gate_ref[0]) * y_ref[0], g_ref[...], b_ref[...])
    x1_ref[0] = x1
    tok = x1 * (1.0 + sc_ref[0]) + sh_ref[0]
    _scatter_lane_tiles(tok_ref, tok, (0,))
    logit = jnp.dot(tok.astype(BF16), wr_ref[...].astype(BF16), preferred_element_type=F32) + br_ref[...]
    lane_i = lax.broadcasted_iota(I32, logit.shape, 1)
    lane = lane_i.astype(F32)
    big = float(LANES)
    neg = -jnp.inf
    gl = jnp.where(lane < n_groups, logit, neg)
    gmax = jnp.max(gl, axis=1, keepdims=True)
    gidx = jnp.min(jnp.where(gl == gmax, lane, big), axis=1, keepdims=True)
    gprob = 1.0 / jnp.sum(jnp.exp(gl - gmax), axis=1, keepdims=True)
    lo = n_groups + gidx * per_group
    el = jnp.where((lane >= lo) & (lane < lo + per_group), logit, neg)
    t1 = jnp.max(el, axis=1, keepdims=True)
    i1 = jnp.min(jnp.where(el == t1, lane, big), axis=1, keepdims=True)
    el2 = jnp.where(lane == i1, neg, el)
    t2 = jnp.max(el2, axis=1, keepdims=True)
    i2 = jnp.min(jnp.where(el2 == t2, lane, big), axis=1, keepdims=True)
    e2 = jnp.exp(t2 - t1)
    w1 = gprob / (1.0 + e2)
    w2 = gprob * e2 / (1.0 + e2)
    eidx_ref[0] = jnp.where(lane_i == 0, i1 - n_groups, jnp.where(lane_i == 1, i2 - n_groups, 0.0)).astype(I32)
    wts_ref[0] = jnp.where(lane_i == 0, w1, jnp.where(lane_i == 1, w2, 0.0))


def ln_route(x3, y3, gate, ln_g, ln_b, sc, sh, w_router, b_router, alpha, n_groups, per_group):
    nb, t, d = x3.shape
    bt = _pick(t, 256)
    blk = pl.BlockSpec((1, bt, d), lambda b, i: (b, i, 0))
    row = pl.BlockSpec((1, d), lambda b, i: (0, 0))
    sel = pl.BlockSpec((1, bt, LANES), lambda b, i: (b, i, 0))
    tiles = pl.BlockSpec((1, bt, d // LANES, LANES), lambda b, i: (b, i, 0, 0))
    kern = functools.partial(_ln_route_kernel, alpha=alpha, n_groups=n_groups, per_group=per_group)
    return pl.pallas_call(
        kern,
        grid=(nb, t // bt),
        in_specs=[blk, blk, _mod_spec(gate, bt), row, row, _mod_spec(sc, bt), _mod_spec(sh, bt),
                  pl.BlockSpec((d, LANES), lambda b, i: (0, 0)), pl.BlockSpec((1, LANES), lambda b, i: (0, 0))],
        out_specs=[blk, tiles, sel, sel],
        out_shape=[jax.ShapeDtypeStruct((nb, t, d), F32), jax.ShapeDtypeStruct((nb, t, d // LANES, LANES), F32),
                   jax.ShapeDtypeStruct((nb, t, LANES), I32), jax.ShapeDtypeStruct((nb, t, LANES), F32)],
        compiler_params=_cparams("arbitrary", "arbitrary"),
        name="ln_route",
    )(x3, y3, gate, ln_g, ln_b, sc, sh, w_router, b_router)


def _ln_combine_kernel(x_ref, y0_ref, y1_ref, wts_ref, gate_ref, g_ref, b_ref, o_ref, *, alpha):
    w = wts_ref[0]
    ffn = w[:, 0:1] * _gather_lane_tiles(y0_ref, (0,)) + w[:, 1:2] * _gather_lane_tiles(y1_ref, (0,))
    o_ref[0] = _layer_norm(alpha * x_ref[0] + (1.0 + gate_ref[0]) * ffn, g_ref[...], b_ref[...])


def ln_combine(x3, y_assign, wts, gate, ln_g, ln_b, alpha):
    nb, t, d = x3.shape
    bt = _pick(t, 256)
    blk = pl.BlockSpec((1, bt, d), lambda b, i: (b, i, 0))
    row = pl.BlockSpec((1, d), lambda b, i: (0, 0))
    ysp = lambda k: pl.BlockSpec((None, 1, bt, d // LANES, LANES), lambda b, i: (k, b, i, 0, 0))
    kern = functools.partial(_ln_combine_kernel, alpha=alpha)
    return pl.pallas_call(
        kern,
        grid=(nb, t // bt),
        in_specs=[blk, ysp(0), ysp(1), pl.BlockSpec((1, bt, LANES), lambda b, i: (b, i, 0)),
                  _mod_spec(gate, bt), row, row],
        out_specs=blk,
        out_shape=jax.ShapeDtypeStruct((nb, t, d), F32),
        compiler_params=_cparams("arbitrary", "arbitrary"),
        name="ln_combine",
    )(x3, y_assign, y_assign, wts, gate, ln_g, ln_b)


def _rank_kernel(e_ref, rank_ref, cnt_ref, car_ref):
    first = (pl.program_id(0) == 0) & (pl.program_id(1) == 0)

    @pl.when(first)
    def _():
        car_ref[...] = jnp.zeros_like(car_ref)

    e = e_ref[...]
    rb = e.shape[1]
    ex = lax.broadcasted_iota(I32, (LANES, rb), 0)
    onehot = jnp.where(ex == e, 1.0, 0.0).astype(BF16)
    r = lax.broadcasted_iota(I32, (rb, rb), 0)
    c = lax.broadcasted_iota(I32, (rb, rb), 1)
    upto = jnp.where(r <= c, 1.0, 0.0).astype(BF16)
    cum = jnp.dot(onehot, upto, preferred_element_type=F32)
    oh = onehot.astype(F32)
    rank = jnp.sum(oh * (cum - 1.0 + car_ref[...]), axis=0, keepdims=True)
    rank_ref[...] = rank.astype(I32)
    car_ref[...] = car_ref[...] + jnp.sum(oh, axis=1, keepdims=True)
    cnt_ref[...] = jnp.broadcast_to(car_ref[...], cnt_ref.shape).astype(I32)


def expert_ranks(e_rows):
    two, nblk, _, rb = e_rows.shape
    return pl.pallas_call(
        _rank_kernel,
        grid=(two, nblk),
        in_specs=[pl.BlockSpec((None, None, 1, rb), lambda k, i: (k, i, 0, 0))],
        out_specs=[pl.BlockSpec((None, None, 1, rb), lambda k, i: (k, i, 0, 0)),
                   pl.BlockSpec((LANES, LANES), lambda k, i: (0, 0))],
        out_shape=[jax.ShapeDtypeStruct(e_rows.shape, I32), jax.ShapeDtypeStruct((LANES, LANES), I32)],
        scratch_shapes=[pltpu.VMEM((LANES, 1), F32)],
        compiler_params=_cparams("arbitrary", "arbitrary"),
        name="expert_ranks",
    )(e_rows)


def _expert_kernel(dst_ref, be_ref, nu_ref, tok_hbm, wg_ref, wu_ref, wd_ref, y_hbm, xbuf, ybuf, gsem, ssem,
                   *, rows, nblk, n_tok):
    del be_ref
    i = pl.program_id(0)
    slot = lax.rem(i, 2)

    def row_copies(blk, sl, gather, wait):
        def body(u, c):
            a = dst_ref[blk * rows + u]

            @pl.when(a >= 0)
            def _():
                if gather:
                    cp = pltpu.make_async_copy(tok_hbm.at[pl.ds(lax.rem(a, n_tok), 1)], xbuf.at[sl, pl.ds(u, 1)],
                                               gsem.at[sl])
                else:
                    cp = pltpu.make_async_copy(ybuf.at[sl, pl.ds(u, 1)], y_hbm.at[pl.ds(a, 1)], ssem.at[sl])
                if wait:
                    cp.wait()
                else:
                    cp.start()

            return c

        lax.fori_loop(0, rows, body, 0, unroll=4)

    @pl.when(i == 0)
    def _():
        xbuf[...] = jnp.zeros_like(xbuf)
        row_copies(0, 0, True, False)

    @pl.when(i + 1 < nblk)
    def _():
        row_copies(i + 1, 1 - slot, True, False)

    row_copies(i, slot, True, True)

    @pl.when(i >= 2)
    def _():
        row_copies(i - 2, slot, False, True)

    @pl.when(i < nu_ref[0])
    def _():
        xb = _gather_lane_tiles(xbuf.at[slot]).astype(BF16)
        g = jnp.dot(xb, wg_ref[...], preferred_element_type=F32)
        u = jnp.dot(xb, wu_ref[...], preferred_element_type=F32)
        h = (g * jax.nn.sigmoid(g) * u).astype(BF16)
        _scatter_lane_tiles(ybuf.at[slot], jnp.dot(h, wd_ref[...], preferred_element_type=F32))

    row_copies(i, slot, False, False)

    @pl.when(i == nblk - 1)
    def _():
        if nblk >= 2:
            row_copies(i - 1, 1 - slot, False, True)
        row_copies(i, slot, False, True)


def expert_blocks(tok, dst_sorted, blk_e, n_used, wg, wu, wd, rows):
    n, nc, _ = tok.shape
    d = nc * LANES
    hid = wg.shape[-1]
    nblk = dst_sorted.shape[0] // rows
    live = lambda i, nu: jnp.minimum(i, nu[0] - 1)
    hbm = pl.BlockSpec(memory_space=pl.ANY)
    kern = functools.partial(_expert_kernel, rows=rows, nblk=nblk, n_tok=n)
    return pl.pallas_call(
        kern,
        grid_spec=pltpu.PrefetchScalarGridSpec(
            num_scalar_prefetch=3,
            grid=(nblk,),
            in_specs=[hbm,
                      pl.BlockSpec((None, d, hid), lambda i, ds, be, nu: (be[live(i, nu)], 0, 0)),
                      pl.BlockSpec((None, d, hid), lambda i, ds, be, nu: (be[live(i, nu)], 0, 0)),
                      pl.BlockSpec((None, hid, d), lambda i, ds, be, nu: (be[live(i, nu)], 0, 0))],
            out_specs=hbm,
            scratch_shapes=[pltpu.VMEM((2, rows, nc, LANES), F32), pltpu.VMEM((2, rows, nc, LANES), F32),
                            pltpu.SemaphoreType.DMA((2,)), pltpu.SemaphoreType.DMA((2,))],
        ),
        out_shape=jax.ShapeDtypeStruct((2 * n, nc, LANES), F32),
        compiler_params=_cparams("arbitrary"),
        name="expert_blocks",
    )(dst_sorted, blk_e, n_used, tok, wg, wu, wd)


def hierarchical_moe(tok3, eidx, wg, wu, wd, rows):
    nb, t, nc, _ = tok3.shape
    n = nb * t
    n_exp = wg.shape[0]
    e2 = eidx.reshape(n, LANES)[:, :2].T
    if n >= LANES:
        rb = _pick(n, 512)
        rank, cnt = expert_ranks(e2.reshape(2, n // rb, 1, rb))
        rank = rank.reshape(-1)
        counts = cnt[:n_exp, 0]
    else:
        ef = e2.reshape(-1)
        ar = jnp.arange(2 * n)
        rank = jnp.sum((ef[:, None] == ef[None, :]) & (ar[None, :] < ar[:, None]), axis=1).astype(I32)
        counts = jnp.sum(ef[:, None] == jnp.arange(n_exp)[None, :], axis=0).astype(I32)
    ef = e2.reshape(-1)
    nblk = min(-(-(2 * n + n_exp * (rows - 1)) // rows), 2 * n)
    padded = (counts + rows - 1) // rows * rows
    pad_end = jnp.cumsum(padded)
    dest = ((pad_end - padded)[ef] + rank).astype(I32)
    dst_sorted = jnp.full((nblk * rows,), -1, I32).at[dest].set(jnp.arange(2 * n, dtype=I32))
    blk_e = jnp.minimum(jnp.sum(pad_end[None, :] <= (jnp.arange(nblk) * rows)[:, None], axis=1), n_exp - 1).astype(I32)
    n_used = (pad_end[-1:] // rows).astype(I32)
    y_assign = expert_blocks(tok3.reshape(n, nc, LANES), dst_sorted, blk_e, n_used, wg, wu, wd, rows)
    return y_assign.reshape(2, nb, t, nc, LANES)


def _dims(p):
    d = p['w_o'].shape[-1]
    aw = p['w_br_attn'].shape[1]
    cw = p['w_br_conv'].shape[1]
    sw = p['w_br_ssm'].shape[1]
    return d, aw, cw, sw


def _mixer_common(xin, p, layer, nb, t, heads, cumsum):
    d, aw, cw, sw = _dims(p)
    wt = p['w_in_t']
    base = layer * (wt.shape[0] // p['w_o'].shape[0])
    zqkv = matmul_nt(xin, wt, base, 3 * aw)
    zrest = matmul_nt(xin, wt, base + 3 * aw + heads, 3 * cw + sw + 3 * d)
    b_pad = jnp.zeros((1, LANES), F32).at[0, :heads].set(p['b_forget'][layer])
    logf, fcum = forget_gate(xin, wt, base + 3 * aw, b_pad, nb, t, cumsum)
    return zqkv, zrest, logf, fcum


def _finish_mixer(x3, ya, yc, ys, zrest, p, layer, mods, alpha, n_groups, per_group, w_router, b_router):
    d, aw, cw, sw = _dims(p)
    nb, t, _ = x3.shape
    merged = merge_branches(ya, yc, ys, zrest, 3 * cw + sw, p['w_br_attn'], p['w_br_conv'], p['w_br_ssm'], layer, d)
    mix = matmul(merged, p['w_o'], layer)
    sh1, sc1, g1, sh2, sc2, g2 = mods
    return ln_route(x3, mix.reshape(nb, t, d), g1, p['ln1_g'][layer][None], p['ln1_b'][layer][None], sc2, sh2,
                    w_router, b_router, alpha, n_groups, per_group)


def _ssm_tiles(state, nt):
    nb = state.shape[0]
    return state.reshape(nb, nt, -1).transpose(1, 0, 2)


def _ssm_untile(h, nb, g, p):
    return h.transpose(1, 0, 2).reshape(nb, g, p)


def kernel(x_prompt, x_sample, c_prompt, c_sample, cache_k, cache_v, cache_logf, page_table, state_conv, state_ssm_re, state_ssm_im, w_ada, b_ada, ln1_g, ln1_b, ln2_g, ln2_b, w_in, b_forget, conv_w, ssm_lambda_re, ssm_lambda_im, ssm_log_dt, ssm_b_re, ssm_b_im, ssm_c_re, ssm_c_im, ssm_d, ssm_w_glu, w_br_attn, w_br_conv, w_br_ssm, w_o, router_w_group, router_b_group, router_w_expert, router_b_expert, moe_w_gate, moe_w_up, moe_w_down):
    p = dict(b_forget=b_forget, w_br_attn=w_br_attn, w_br_conv=w_br_conv, w_br_ssm=w_br_ssm, w_o=w_o,
             ln1_g=ln1_g, ln1_b=ln1_b, ln2_g=ln2_g, ln2_b=ln2_b)
    depth = w_in.shape[0]
    nbp, t, d = x_prompt.shape
    nbs, ts, _ = x_sample.shape
    assert ts == 1
    heads, dh = cache_k.shape[3], cache_k.shape[4]
    aw = heads * dh
    cw = conv_w.shape[-1]
    sw = ssm_d.shape[-1]
    g, pst = ssm_lambda_re.shape[1], ssm_lambda_re.shape[2]
    gc = ssm_b_re.shape[-1]
    nt = g * gc // LANES
    ns = (LANES // gc) * pst
    n_groups = router_w_group.shape[-1]
    n_exp = router_w_expert.shape[-1]
    per_group = n_exp // n_groups
    alpha = (2 * depth) ** 0.25
    chunk = SSM_CHUNK
    col_conv = 0
    col_u = 3 * cw

    r = nbp + nbs
    rpad = -(-r // 8) * 8
    c_all = jnp.zeros((rpad, d), F32).at[:nbp].set(c_prompt).at[nbp:r].set(c_sample)
    mod_all = ada_modulation(c_all, w_ada, b_ada)

    assert w_in.shape[2] % 8 == 0 and (3 * aw + heads) % 8 == 0
    p['w_in_t'] = jnp.swapaxes(w_in, 1, 2).reshape(depth * w_in.shape[2], d)

    wg_bf = moe_w_gate.astype(BF16)
    wu_bf = moe_w_up.astype(BF16)
    wd_bf = moe_w_down.astype(BF16)

    xp = x_prompt
    xs = x_sample.reshape(1, nbs, d)
    outs = {k: [] for k in ('kp', 'vp', 'fp', 'ks', 'vs', 'fs', 'cp', 'cs', 'srp', 'sip', 'srs', 'sis')}
    zero_conv = jnp.zeros((nbp, 2, cw), F32)
    zero_h = jnp.zeros((nt, nbp, 2 * ns), F32)
    for l in range(depth):
        mp = [mod_all[l, :nbp, i * d:(i + 1) * d].reshape(nbp, 1, d) for i in range(6)]
        ms = [mod_all[l, nbp:r, i * d:(i + 1) * d].reshape(1, nbs, d) for i in range(6)]
        w_router = jnp.zeros((d, LANES), F32).at[:, :n_groups].set(router_w_group[l]) \
            .at[:, n_groups:n_groups + n_exp].set(router_w_expert[l])
        b_router = jnp.zeros((1, LANES), F32).at[0, :n_groups].set(router_b_group[l]) \
            .at[0, n_groups:n_groups + n_exp].set(router_b_expert[l])
        wts = ssm_weights(ssm_lambda_re[l], ssm_lambda_im[l], ssm_log_dt[l], ssm_b_re[l], ssm_b_im[l],
                          ssm_c_re[l], ssm_c_im[l], chunk)
        d_t = ssm_d[l].reshape(nt, 1, LANES)
        d_cat = jnp.tile(d_t, (1, 1, chunk))
        eye = jnp.eye(LANES // gc, dtype=F32)
        cdiag = lambda c: jnp.einsum('tgcp,gh->thcgp', c.reshape(nt, LANES // gc, gc, pst), eye).reshape(nt, LANES, ns)
        cmat = jnp.concatenate([cdiag(ssm_c_re[l]), -cdiag(ssm_c_im[l])], axis=2).astype(BF16)

        n = nbp * t
        xin = modulate(xp, mp[1], mp[0]).reshape(n, d)
        zqkv, zrest, logf, fcum = _mixer_common(xin, p, l, nbp, t, heads, True)
        fcum_t = fcum[:, :heads].reshape(nbp, t, heads).transpose(0, 2, 1)
        ya = flash_attention(zqkv, fcum, fcum_t, nbp, t, heads, dh)
        yc, conv_p = short_conv_prompt(zrest, col_conv, conv_w[l], zero_conv, nbp, t, cw)
        zs, hfin = ssm_prompt(zrest.reshape(nbp, t, -1), col_u, wts, zero_h, d_cat, chunk)
        ys = glu(zs.reshape(n, sw), ssm_w_glu, l)
        x1, tok, eidx, wsel = _finish_mixer(xp, ya, yc, ys, zrest, p, l, mp, alpha, n_groups, per_group,
                                            w_router, b_router)
        y_assign = hierarchical_moe(tok, eidx, wg_bf[l], wu_bf[l], wd_bf[l], MOE_ROWS)
        xp = ln_combine(x1, y_assign, wsel, mp[5], ln2_g[l][None], ln2_b[l][None], alpha)
        outs['kp'].append(zqkv[:, aw:2 * aw].reshape(nbp, t, heads, dh))
        outs['vp'].append(zqkv[:, 2 * aw:].reshape(nbp, t, heads, dh))
        outs['fp'].append(logf[:, :heads].reshape(nbp, t, heads))
        outs['cp'].append(conv_p)
        outs['srp'].append(_ssm_untile(hfin[:, :, :ns], nbp, g, pst))
        outs['sip'].append(_ssm_untile(hfin[:, :, ns:], nbp, g, pst))

        xin_s = modulate(xs, ms[1], ms[0]).reshape(nbs, d)
        zqkv_s, zrest_s, logf_s, _ = _mixer_common(xin_s, p, l, 1, nbs, heads, False)
        q_s = zqkv_s[:, :aw].reshape(nbs, heads, dh)
        k_s = zqkv_s[:, aw:2 * aw].reshape(nbs, heads, dh)
        v_s = zqkv_s[:, 2 * aw:].reshape(nbs, heads, dh)
        ya_s = decode_attention(q_s, k_s, v_s, logf_s[:, :heads].reshape(nbs, 1, heads), cache_k, cache_v, cache_logf,
                                page_table, l)
        ya_s = ya_s.reshape(nbs, aw).astype(BF16)
        yc_s, v_row = short_conv_step(zrest_s, col_conv, conv_w[l], state_conv[l, :, 0], state_conv[l, :, 1], cw)
        h0 = jnp.concatenate([_ssm_tiles(state_ssm_re[l], nt), _ssm_tiles(state_ssm_im[l], nt)], axis=2)
        zs_s, h_s = ssm_step(zrest_s, col_u, wts, cmat, h0, d_t, chunk)
        ys_s = glu(zs_s, ssm_w_glu, l)
        x1_s, tok_s, eidx_s, wsel_s = _finish_mixer(xs, ya_s, yc_s, ys_s, zrest_s, p, l, ms, alpha, n_groups,
                                                    per_group, w_router, b_router)
        y_assign_s = hierarchical_moe(tok_s, eidx_s, wg_bf[l], wu_bf[l], wd_bf[l], 8)
        xs = ln_combine(x1_s, y_assign_s, wsel_s, ms[5], ln2_g[l][None], ln2_b[l][None], alpha)
        outs['ks'].append(k_s.reshape(nbs, 1, heads, dh))
        outs['vs'].append(v_s.reshape(nbs, 1, heads, dh))
        outs['fs'].append(logf_s[:, :heads].reshape(nbs, 1, heads))
        outs['cs'].append(jnp.stack([state_conv[l, :, 1], v_row], axis=1))
        outs['srs'].append(_ssm_untile(h_s[:, :, :ns], nbs, g, pst))
        outs['sis'].append(_ssm_untile(h_s[:, :, ns:], nbs, g, pst))

    st = lambda k: jnp.stack(outs[k])
    return (xp, xs.reshape(nbs, 1, d), st('kp'), st('vp'), st('fp'), st('ks'), st('vs'), st('fs'),
            st('cp'), st('cs'), st('srp'), st('sip'), st('srs'), st('sis'))
```

```python
import functools
import math

import jax
import jax.numpy as jnp
from jax import lax
from jax.experimental import pallas as pl
from jax.experimental.pallas import tpu as pltpu

F32 = jnp.float32
BF16 = jnp.bfloat16
I32 = jnp.int32

LANES = 128
VMEM_LIMIT = 56 * 1024 * 1024
LN_EPS = 1e-5
SSM_CHUNK = 8
MOE_ROWS = 128
NT_DIMS = (((1,), (1,)), ((), ()))


def _cparams(*sem):
    return pltpu.CompilerParams(dimension_semantics=sem, vmem_limit_bytes=VMEM_LIMIT)


def _pick(n, pref):
    if n <= pref:
        return n
    b = pref
    while n % b:
        b //= 2
    return b


def _ada_kernel(c_ref, w_ref, b_ref, o_ref):
    c = c_ref[...]
    s = (c * jax.nn.sigmoid(c)).astype(BF16)
    o_ref[...] = jnp.dot(s, w_ref[...].astype(BF16), preferred_element_type=F32) + b_ref[...]


def ada_modulation(c_all, w_ada, b_ada):
    depth, d, n6 = w_ada.shape
    r = c_all.shape[0]
    bn = _pick(n6, 512)
    return pl.pallas_call(
        _ada_kernel,
        grid=(depth, n6 // bn),
        in_specs=[
            pl.BlockSpec((r, d), lambda l, j: (0, 0)),
            pl.BlockSpec((None, d, bn), lambda l, j: (l, 0, j)),
            pl.BlockSpec((None, 1, bn), lambda l, j: (l, 0, j)),
        ],
        out_specs=pl.BlockSpec((None, r, bn), lambda l, j: (l, 0, j)),
        out_shape=jax.ShapeDtypeStruct((depth, r, n6), F32),
        compiler_params=_cparams("arbitrary", "arbitrary"),
        name="ada_modulation",
    )(c_all, w_ada, b_ada.reshape(depth, 1, n6))


def _mod_spec(mod, bt):
    if mod.shape[1] == 1:
        return pl.BlockSpec((1, 1, mod.shape[2]), lambda b, t: (b, 0, 0))
    return pl.BlockSpec((1, bt, mod.shape[2]), lambda b, t: (b, t, 0))


def _modulate_kernel(x_ref, sc_ref, sh_ref, o_ref):
    o_ref[...] = (x_ref[...] * (1.0 + sc_ref[...]) + sh_ref[...]).astype(o_ref.dtype)


def modulate(x3, sc, sh):
    nb, t, d = x3.shape
    bt = _pick(t, 512)
    return pl.pallas_call(
        _modulate_kernel,
        grid=(nb, t // bt),
        in_specs=[pl.BlockSpec((1, bt, d), lambda b, i: (b, i, 0)), _mod_spec(sc, bt), _mod_spec(sh, bt)],
        out_specs=pl.BlockSpec((1, bt, d), lambda b, i: (b, i, 0)),
        out_shape=jax.ShapeDtypeStruct((nb, t, d), BF16),
        compiler_params=_cparams("arbitrary", "arbitrary"),
        name="modulate",
    )(x3, sc, sh)


def _mm_kernel(a_ref, w_ref, o_ref, wbf_ref):
    @pl.when(pl.program_id(1) == 0)
    def _():
        wbf_ref[...] = w_ref[...].astype(BF16)

    o_ref[...] = jnp.dot(a_ref[...].astype(BF16), wbf_ref[...], preferred_element_type=F32).astype(o_ref.dtype)


def matmul(a, w3, layer, *, out_dtype=F32, bm=1024, bn=512):
    m, k = a.shape
    ncols = w3.shape[2]
    bm = _pick(m, bm)
    bn = _pick(ncols, bn)
    assert ncols % bn == 0 and m % bm == 0
    return pl.pallas_call(
        _mm_kernel,
        grid=(ncols // bn, m // bm),
        in_specs=[
            pl.BlockSpec((bm, k), lambda j, i: (i, 0)),
            pl.BlockSpec((None, k, bn), lambda j, i: (layer, 0, j)),
        ],
        out_specs=pl.BlockSpec((bm, bn), lambda j, i: (i, j)),
        out_shape=jax.ShapeDtypeStruct((m, ncols), out_dtype),
        scratch_shapes=[pltpu.VMEM((k, bn), BF16)],
        compiler_params=_cparams("arbitrary", "arbitrary"),
        name="matmul",
    )(a, w3)


def _mm_nt_kernel(a_ref, w_ref, o_ref, wbf_ref):
    @pl.when(pl.program_id(1) == 0)
    def _():
        wbf_ref[...] = w_ref[...].astype(BF16)

    o_ref[...] = lax.dot_general(a_ref[...].astype(BF16), wbf_ref[...], NT_DIMS,
                                 preferred_element_type=F32).astype(o_ref.dtype)


def _weight_rows(row0, bn, k):
    return pl.BlockSpec((pl.Element(bn), pl.Element(k)), lambda j, i: (pl.multiple_of(row0 + j * bn, 8), 0))


def matmul_nt(a, wt, row0, ncols, *, out_dtype=F32, bm=1024, bn=512):
    m, k = a.shape
    bm = _pick(m, bm)
    bn = _pick(ncols, bn)
    assert ncols % bn == 0 and m % bm == 0 and row0 % 8 == 0 and row0 + ncols <= wt.shape[0]
    return pl.pallas_call(
        _mm_nt_kernel,
        grid=(ncols // bn, m // bm),
        in_specs=[pl.BlockSpec((bm, k), lambda j, i: (i, 0)), _weight_rows(row0, bn, k)],
        out_specs=pl.BlockSpec((bm, bn), lambda j, i: (i, j)),
        out_shape=jax.ShapeDtypeStruct((m, ncols), out_dtype),
        scratch_shapes=[pltpu.VMEM((bn, k), BF16)],
        compiler_params=_cparams("arbitrary", "arbitrary"),
        name="matmul_nt",
    )(a, wt)


def _forget_kernel(a_ref, w_ref, b_ref, lf_ref, fc_ref, carry_ref, *, cumsum):
    t = pl.program_id(1)
    z = lax.dot_general(a_ref[...].astype(BF16), w_ref[...].astype(BF16), NT_DIMS,
                        preferred_element_type=F32) + b_ref[...]
    lf = jnp.minimum(z, 0.0) - jnp.log1p(jnp.exp(-jnp.abs(z)))
    lf_ref[...] = lf
    if not cumsum:
        fc_ref[...] = lf
        return

    @pl.when(t == 0)
    def _():
        carry_ref[...] = jnp.zeros_like(carry_ref)

    bt = lf.shape[0]
    row = lax.broadcasted_iota(I32, lf.shape, 0)
    acc = lf
    s = 1
    while s < bt:
        acc = acc + jnp.where(row >= s, pltpu.roll(acc, s, axis=0), 0.0)
        s *= 2
    acc = acc + carry_ref[...]
    fc_ref[...] = acc
    carry_ref[...] = acc[bt - 1:bt, :]


def forget_gate(xin, wt, row0, b_pad, nb, t, cumsum):
    n, d = xin.shape
    bt = _pick(t, 512)
    nt = t // bt
    assert row0 % 8 == 0 and row0 + LANES <= wt.shape[0]
    kern = functools.partial(_forget_kernel, cumsum=cumsum)
    return pl.pallas_call(
        kern,
        grid=(nb, nt),
        in_specs=[
            pl.BlockSpec((bt, d), lambda b, i: (b * nt + i, 0)),
            pl.BlockSpec((pl.Element(LANES), pl.Element(d)), lambda b, i: (row0, 0)),
            pl.BlockSpec((1, LANES), lambda b, i: (0, 0)),
        ],
        out_specs=[pl.BlockSpec((bt, LANES), lambda b, i: (b * nt + i, 0))] * 2,
        out_shape=[jax.ShapeDtypeStruct((n, LANES), F32)] * 2,
        scratch_shapes=[pltpu.VMEM((1, LANES), F32)],
        compiler_params=_cparams("arbitrary", "arbitrary"),
        name="forget_gate",
    )(xin, wt, b_pad)


def _flash_kernel(qi_ref, ki_ref, q_ref, k_ref, v_ref, fq_ref, fk_ref, o_ref, m_ref, l_ref, acc_ref, fqc_ref,
                  *, hp, dh):
    hb = pl.program_id(1) * hp
    step = pl.program_id(2)
    qi = qi_ref[step]
    ki = ki_ref[step]

    @pl.when(ki == 0)
    def _():
        m_ref[...] = jnp.full_like(m_ref, -jnp.inf)
        l_ref[...] = jnp.zeros_like(l_ref)
        acc_ref[...] = jnp.zeros_like(acc_ref)
        lane = lax.broadcasted_iota(I32, fq_ref.shape, 1)
        for hh in range(hp):
            fqc_ref[hh] = jnp.sum(jnp.where(lane == hb + hh, fq_ref[...], 0.0), axis=1, keepdims=True)

    def block(masked):
        for hh in range(hp):
            cols = slice(hh * dh, (hh + 1) * dh)
            s = lax.dot_general(q_ref[:, cols].astype(BF16), k_ref[:, cols].astype(BF16), NT_DIMS,
                                preferred_element_type=F32) * (dh ** -0.5)
            s = s + fqc_ref[hh] - fk_ref[pl.ds(hb + hh, 1), :]
            if masked:
                row = lax.broadcasted_iota(I32, s.shape, 0)
                col = lax.broadcasted_iota(I32, s.shape, 1)
                s = jnp.where(col <= row, s, -jnp.inf)
            m_prev = m_ref[hh]
            m_new = jnp.maximum(m_prev, jnp.max(s, axis=1, keepdims=True))
            alpha = jnp.exp(m_prev - m_new)
            p = jnp.exp(s - m_new)
            l_ref[hh] = alpha * l_ref[hh] + jnp.sum(p, axis=1, keepdims=True)
            acc_ref[hh] = alpha * acc_ref[hh] + jnp.dot(p.astype(BF16), v_ref[:, cols].astype(BF16),
                                                        preferred_element_type=F32)
            m_ref[hh] = m_new

    @pl.when(ki < qi)
    def _():
        block(False)

    @pl.when(ki == qi)
    def _():
        block(True)
        for hh in range(hp):
            o_ref[:, hh * dh:(hh + 1) * dh] = (acc_ref[hh] / l_ref[hh]).astype(o_ref.dtype)


def flash_attention(zqkv, fcum, fcum_t, nb, t, heads, dh):
    n = nb * t
    bq = _pick(t, 512)
    nq = t // bq
    hp = 2 if heads % 2 == 0 else 1
    ng = heads // hp
    pairs = [(i, j) for i in range(nq) for j in range(i + 1)]
    qi_tab = jnp.asarray([p[0] for p in pairs], I32)
    ki_tab = jnp.asarray([p[1] for p in pairs], I32)
    kern = functools.partial(_flash_kernel, hp=hp, dh=dh)
    w = hp * dh
    return pl.pallas_call(
        kern,
        grid_spec=pltpu.PrefetchScalarGridSpec(
            num_scalar_prefetch=2,
            grid=(nb, ng, len(pairs)),
            in_specs=[
                pl.BlockSpec((bq, w), lambda b, g, s, qt, kt: (b * nq + qt[s], g)),
                pl.BlockSpec((bq, w), lambda b, g, s, qt, kt: (b * nq + kt[s], ng + g)),
                pl.BlockSpec((bq, w), lambda b, g, s, qt, kt: (b * nq + kt[s], 2 * ng + g)),
                pl.BlockSpec((bq, LANES), lambda b, g, s, qt, kt: (b * nq + qt[s], 0)),
                pl.BlockSpec((None, heads, bq), lambda b, g, s, qt, kt: (b, 0, kt[s])),
            ],
            out_specs=pl.BlockSpec((bq, w), lambda b, g, s, qt, kt: (b * nq + qt[s], g)),
            scratch_shapes=[pltpu.VMEM((hp, bq, 1), F32), pltpu.VMEM((hp, bq, 1), F32),
                            pltpu.VMEM((hp, bq, dh), F32), pltpu.VMEM((hp, bq, 1), F32)],
        ),
        out_shape=jax.ShapeDtypeStruct((n, heads * dh), BF16),
        compiler_params=_cparams("arbitrary", "arbitrary", "arbitrary"),
        name="flash_attention",
    )(qi_tab, ki_tab, zqkv, zqkv, zqkv, fcum, fcum_t)


def _decode_kernel(pt_ref, q_ref, kn_ref, vn_ref, ln_ref, lp_hbm, k_hbm, v_hbm, o_ref,
                   lp_buf, k_buf, v_buf, sem, m_ref, l_ref, acc_ref, car_ref,
                   *, layer, scale, n_seq, n_pages, group):
    b = pl.program_id(0)
    i = pl.program_id(1)
    n_steps = n_pages // group
    step = b * n_steps + i
    slot = lax.rem(step, 2)

    def page_copies(seq, st, sl):
        out = []
        for g in range(group):
            pg = pt_ref[seq * n_pages + (n_pages - 1 - (st * group + g))]
            out.append(pltpu.make_async_copy(lp_hbm.at[layer, pg], lp_buf.at[sl, g], sem.at[sl]))
            out.append(pltpu.make_async_copy(k_hbm.at[layer, pg], k_buf.at[sl, g], sem.at[sl]))
            out.append(pltpu.make_async_copy(v_hbm.at[layer, pg], v_buf.at[sl, g], sem.at[sl]))
        return out

    @pl.when(step == 0)
    def _():
        for cp in page_copies(b, i, slot):
            cp.start()

    @pl.when(step + 1 < n_seq * n_steps)
    def _():
        wrap = i + 1 == n_steps
        for cp in page_copies(jnp.where(wrap, b + 1, b), jnp.where(wrap, 0, i + 1), 1 - slot):
            cp.start()

    for cp in page_copies(b, i, slot):
        cp.wait()
    lp_refs = [lp_buf.at[slot, g] for g in range(group)]
    k_refs = [k_buf.at[slot, g] for g in range(group)]
    v_refs = [v_buf.at[slot, g] for g in range(group)]

    @pl.when(i == 0)
    def _():
        m_ref[...] = jnp.full_like(m_ref, -jnp.inf)
        l_ref[...] = jnp.zeros_like(l_ref)
        acc_ref[...] = jnp.zeros_like(acc_ref)
        car_ref[...] = ln_ref[...]

    q = q_ref[...]
    qs = q * scale
    page = lp_refs[0].shape[0]
    r = lax.broadcasted_iota(I32, (page, page), 0)
    c = lax.broadcasted_iota(I32, (page, page), 1)
    after = jnp.where(c > r, 1.0, 0.0).astype(F32)
    carry = car_ref[...]
    scores = []
    for g in range(group):
        lp = lp_refs[g][...]
        suffix = jnp.dot(after, lp, preferred_element_type=F32, precision=lax.Precision.HIGHEST) + carry
        carry = suffix[0:1, :] + lp[0:1, :]
        scores.append(jnp.sum(k_refs[g][...] * qs[None], axis=-1, keepdims=True) + suffix[:, :, None])
    car_ref[...] = carry
    m_prev = m_ref[...]
    m_new = m_prev
    for s in scores:
        m_new = jnp.maximum(m_new, jnp.max(s, axis=0))
    alpha = jnp.exp(m_prev - m_new)
    l_new = alpha * l_ref[...]
    acc = alpha * acc_ref[...]
    for g in range(group):
        p = jnp.exp(scores[g] - m_new[None])
        l_new = l_new + jnp.sum(p, axis=0)
        acc = acc + jnp.sum(p * v_refs[g][...], axis=0)
    l_ref[...] = l_new
    acc_ref[...] = acc
    m_ref[...] = m_new

    @pl.when(i == n_steps - 1)
    def _():
        s_new = jnp.sum(qs * kn_ref[...], axis=-1, keepdims=True)
        m_fin = jnp.maximum(m_new, s_new)
        a = jnp.exp(m_new - m_fin)
        p_new = jnp.exp(s_new - m_fin)
        o_ref[...] = (a * acc + p_new * vn_ref[...]) / (a * l_new + p_new)


def decode_attention(q, k_new, v_new, logf_new, cache_k, cache_v, cache_logf, page_table, layer):
    nb, heads, dh = q.shape
    page = cache_k.shape[2]
    n_pages = page_table.shape[1]
    group = _pick(n_pages, 4)
    n_steps = n_pages // group
    pt = page_table.reshape(-1).astype(I32)
    kern = functools.partial(_decode_kernel, layer=layer, scale=dh ** -0.5, n_seq=nb, n_pages=n_pages, group=group)
    tok = pl.BlockSpec((None, heads, dh), lambda b, i, pt_ref: (b, 0, 0))
    hbm = pl.BlockSpec(memory_space=pl.ANY)
    return pl.pallas_call(
        kern,
        grid_spec=pltpu.PrefetchScalarGridSpec(
            num_scalar_prefetch=1,
            grid=(nb, n_steps),
            in_specs=[tok, tok, tok, pl.BlockSpec((None, 1, heads), lambda b, i, pt_ref: (b, 0, 0)), hbm, hbm, hbm],
            out_specs=tok,
            scratch_shapes=[pltpu.VMEM((2, group, page, heads), F32),
                            pltpu.VMEM((2, group, page, heads, dh), F32),
                            pltpu.VMEM((2, group, page, heads, dh), F32),
                            pltpu.SemaphoreType.DMA((2,)),
                            pltpu.VMEM((heads, 1), F32), pltpu.VMEM((heads, 1), F32), pltpu.VMEM((heads, dh), F32),
                            pltpu.VMEM((1, heads), F32)],
        ),
        out_shape=jax.ShapeDtypeStruct((nb, heads, dh), F32),
        compiler_params=_cparams("arbitrary", "arbitrary"),
        name="decode_attention",
    )(pt, q, k_new, v_new, logf_new, cache_logf, cache_k, cache_v)


def _conv_kernel(b_ref, c_ref, x_ref, w_ref, prev_ref, y_ref, st_ref, car_ref):
    t = pl.program_id(2)
    v = c_ref[...] * x_ref[...]
    bt = v.shape[0]

    @pl.when(t == 0)
    def _():
        car_ref[...] = prev_ref[0]

    p0 = car_ref[0:1, :]
    p1 = car_ref[1:2, :]
    row = lax.broadcasted_iota(I32, v.shape, 0)
    r1 = jnp.where(row == 0, p1, pltpu.roll(v, 1, axis=0))
    r2 = jnp.where(row == 0, p0, jnp.where(row == 1, p1, pltpu.roll(v, 2, axis=0)))
    w = w_ref[...]
    y = w[0:1, :] * r2 + w[1:2, :] * r1 + w[2:3, :] * v
    y_ref[...] = (b_ref[...] * y).astype(y_ref.dtype)
    tail = v[bt - 2:bt, :]
    car_ref[...] = tail
    st_ref[0] = tail


def short_conv_prompt(zrest, col_b, conv_w_l, prev, nb, t, cw):
    n = nb * t
    bt = _pick(t, 512)
    bc = _pick(cw, 512)
    nt = t // bt
    o = col_b // bc
    per = cw // bc
    zspec = lambda k: pl.BlockSpec((bt, bc), lambda b, c, i: (b * nt + i, o + k * per + c))
    return pl.pallas_call(
        _conv_kernel,
        grid=(nb, per, nt),
        in_specs=[zspec(0), zspec(1), zspec(2),
                  pl.BlockSpec((3, bc), lambda b, c, i: (0, c)),
                  pl.BlockSpec((1, 2, bc), lambda b, c, i: (b, 0, c))],
        out_specs=[pl.BlockSpec((bt, bc), lambda b, c, i: (b * nt + i, c)),
                   pl.BlockSpec((1, 2, bc), lambda b, c, i: (b, 0, c))],
        out_shape=[jax.ShapeDtypeStruct((n, cw), BF16), jax.ShapeDtypeStruct((nb, 2, cw), F32)],
        scratch_shapes=[pltpu.VMEM((2, bc), F32)],
        compiler_params=_cparams("arbitrary", "arbitrary", "arbitrary"),
        name="short_conv_prompt",
    )(zrest, zrest, zrest, conv_w_l, prev)


def _conv_step_kernel(b_ref, c_ref, x_ref, w_ref, p0_ref, p1_ref, y_ref, v_ref):
    v = c_ref[...] * x_ref[...]
    w = w_ref[...]
    y = w[0:1, :] * p0_ref[...] + w[1:2, :] * p1_ref[...] + w[2:3, :] * v
    y_ref[...] = (b_ref[...] * y).astype(y_ref.dtype)
    v_ref[...] = v


def short_conv_step(zrest, col_b, conv_w_l, prev0, prev1, cw):
    nb = zrest.shape[0]
    bc = _pick(cw, 512)
    o = col_b // bc
    per = cw // bc
    zspec = lambda k: pl.BlockSpec((nb, bc), lambda c: (0, o + k * per + c))
    vec = pl.BlockSpec((nb, bc), lambda c: (0, c))
    return pl.pallas_call(
        _conv_step_kernel,
        grid=(per,),
        in_specs=[zspec(0), zspec(1), zspec(2), pl.BlockSpec((3, bc), lambda c: (0, c)), vec, vec],
        out_specs=[vec, vec],
        out_shape=[jax.ShapeDtypeStruct((nb, cw), BF16), jax.ShapeDtypeStruct((nb, cw), F32)],
        compiler_params=_cparams("arbitrary"),
        name="short_conv_step",
    )(zrest, zrest, zrest, conv_w_l, prev0, prev1)


def _ssm_weights_kernel(lre_ref, lim_ref, ldt_ref, bre_ref, bim_ref, cre_ref, cim_ref,
                        t_ref, s_ref, o_ref, al_ref, a1_ref, *, chunk):
    ns = lre_ref.shape[1]
    lr = lre_ref[...]
    li = lim_ref[...]
    dt = jnp.exp(ldt_ref[...])
    mag = jnp.exp(lr * dt)
    ang = li * dt
    ar = mag * jnp.cos(ang)
    ai = mag * jnp.sin(ang)
    den = lr * lr + li * li
    nr = ar - 1.0
    fr = (nr * lr + ai * li) / den
    fi = (ai * lr - nr * li) / den
    bre = bre_ref[...]
    bim = bim_ref[...]
    bbr = fr * bre - fi * bim
    bbi = fr * bim + fi * bre
    cre = cre_ref[...]
    cim = cim_ref[...]
    cfull = jnp.concatenate([cre, -cim], axis=1)
    pows = []
    pr = jnp.ones_like(ar)
    pi = jnp.zeros_like(ar)
    for _ in range(chunk + 1):
        pows.append((pr, pi))
        pr, pi = pr * ar - pi * ai, pr * ai + pi * ar
    a1_ref[...] = jnp.concatenate([ar, ai], axis=1)
    al_ref[...] = jnp.concatenate(list(pows[chunk]), axis=1)
    taps = [None] * chunk
    for j in range(chunk):
        qr, qi = pows[chunk - 1 - j]
        blk = jnp.concatenate([qr * bbr - qi * bbi, qr * bbi + qi * bbr], axis=1)
        s_ref[j * LANES:(j + 1) * LANES, :] = blk.astype(s_ref.dtype)
        taps[chunk - 1 - j] = lax.dot_general(blk, cfull, NT_DIMS, preferred_element_type=F32,
                                              precision=lax.Precision.HIGHEST)
    zero = jnp.zeros((LANES, LANES), F32)
    for j in range(chunk):
        for i in range(chunk):
            t_ref[j * LANES:(j + 1) * LANES, i * LANES:(i + 1) * LANES] = (
                taps[i - j] if i >= j else zero).astype(t_ref.dtype)
    for i in range(chunk):
        qr, qi = pows[i + 1]
        o_ref[i * LANES:(i + 1) * LANES, :] = jnp.concatenate(
            [cre * qr - cim * qi, -cre * qi - cim * qr], axis=1).astype(o_ref.dtype)


def ssm_weights(lam_re, lam_im, log_dt, b_re, b_im, c_re, c_im, chunk):
    g, p = lam_re.shape
    gc = b_re.shape[-1]
    gpt = LANES // gc
    nt = g // gpt
    ns = gpt * p
    eye = jnp.eye(gpt, dtype=F32)
    tile = lambda x: x.reshape(nt, 1, ns)
    ldt = tile(jnp.broadcast_to(log_dt[:, None], (g, p)))
    bdiag = lambda b: jnp.einsum('tgpc,gh->thcgp', b.reshape(nt, gpt, p, gc), eye).reshape(nt, LANES, ns)
    cdiag = lambda c: jnp.einsum('tgcp,gh->thcgp', c.reshape(nt, gpt, gc, p), eye).reshape(nt, LANES, ns)
    row = pl.BlockSpec((None, 1, ns), lambda j: (j, 0, 0))
    mat = pl.BlockSpec((None, LANES, ns), lambda j: (j, 0, 0))
    lc = chunk * LANES
    kern = functools.partial(_ssm_weights_kernel, chunk=chunk)
    return pl.pallas_call(
        kern,
        grid=(nt,),
        in_specs=[row, row, row, mat, mat, mat, mat],
        out_specs=[pl.BlockSpec((None, lc, lc), lambda j: (j, 0, 0)),
                   pl.BlockSpec((None, lc, 2 * ns), lambda j: (j, 0, 0)),
                   pl.BlockSpec((None, lc, 2 * ns), lambda j: (j, 0, 0)),
                   pl.BlockSpec((None, 1, 2 * ns), lambda j: (j, 0, 0)),
                   pl.BlockSpec((None, 1, 2 * ns), lambda j: (j, 0, 0))],
        out_shape=[jax.ShapeDtypeStruct((nt, lc, lc), BF16),
                   jax.ShapeDtypeStruct((nt, lc, 2 * ns), BF16),
                   jax.ShapeDtypeStruct((nt, lc, 2 * ns), BF16),
                   jax.ShapeDtypeStruct((nt, 1, 2 * ns), F32),
                   jax.ShapeDtypeStruct((nt, 1, 2 * ns), F32)],
        compiler_params=_cparams("arbitrary"),
        name="ssm_weights",
    )(tile(lam_re), tile(lam_im), ldt, bdiag(b_re), bdiag(b_im), cdiag(c_re), cdiag(c_im))


def _ssm_kernel(u_ref, h0_ref, t_ref, s_ref, o_ref, al_ref, d_ref, z_ref, hf_ref, ucat_ref, sloc_ref, sprev_ref,
                *, chunk):
    nb, t, _ = u_ref.shape
    tc = t // chunk
    ns = al_ref.shape[1] // 2
    for b in range(nb):
        for j in range(chunk):
            ucat_ref[b * tc:(b + 1) * tc, j * LANES:(j + 1) * LANES] = u_ref[b, pl.ds(j, tc, stride=chunk), :]
    nc = ns // LANES
    for b in range(nb):
        rows = slice(b * tc, (b + 1) * tc)
        sl = jnp.dot(ucat_ref[rows, :].astype(BF16), s_ref[...], preferred_element_type=F32)
        for c in range(2 * nc):
            sloc_ref[c, rows, :] = sl[:, c * LANES:(c + 1) * LANES]

    def step(k, h):
        at = pl.ds(k, nb, stride=tc)
        new = [None] * (2 * nc)
        for c in range(nc):
            lanes = slice(c * LANES, (c + 1) * LANES)
            alr = al_ref[:, lanes]
            ali = al_ref[:, ns + c * LANES:ns + (c + 1) * LANES]
            hr = h[c]
            hi = h[nc + c]
            sprev_ref[c, at, :] = hr
            sprev_ref[nc + c, at, :] = hi
            new[c] = alr * hr - ali * hi + sloc_ref[c, at, :]
            new[nc + c] = alr * hi + ali * hr + sloc_ref[nc + c, at, :]
        return tuple(new)

    h0 = h0_ref[...]
    hfin = lax.fori_loop(0, tc, step, tuple(h0[:, c * LANES:(c + 1) * LANES] for c in range(2 * nc)))
    hf_ref[...] = jnp.concatenate(hfin, axis=1)
    for b in range(nb):
        rows = slice(b * tc, (b + 1) * tc)
        uc = ucat_ref[rows, :]
        sp = jnp.concatenate([sprev_ref[c, rows, :] for c in range(2 * nc)], axis=1)
        y = jnp.dot(uc.astype(BF16), t_ref[...], preferred_element_type=F32)
        y = y + lax.dot_general(sp.astype(BF16), o_ref[...], NT_DIMS, preferred_element_type=F32)
        z = jax.nn.gelu(y + d_ref[...] * uc)
        for i in range(chunk):
            z_ref[b, pl.ds(i, tc, stride=chunk), :] = z[:, i * LANES:(i + 1) * LANES]


def ssm_prompt(zrest3, col_u, wts, h0cat, d_cat, chunk):
    tmat, smat, omat, al, _ = wts
    nb, t, _ = zrest3.shape
    nt, lc, ns2 = smat.shape
    tc = t // chunk
    kern = functools.partial(_ssm_kernel, chunk=chunk)
    return pl.pallas_call(
        kern,
        grid=(nt,),
        in_specs=[pl.BlockSpec((nb, t, LANES), lambda j: (0, 0, col_u // LANES + j)),
                  pl.BlockSpec((None, nb, ns2), lambda j: (j, 0, 0)),
                  pl.BlockSpec((None, lc, lc), lambda j: (j, 0, 0)),
                  pl.BlockSpec((None, lc, ns2), lambda j: (j, 0, 0)),
                  pl.BlockSpec((None, lc, ns2), lambda j: (j, 0, 0)),
                  pl.BlockSpec((None, 1, ns2), lambda j: (j, 0, 0)),
                  pl.BlockSpec((None, 1, lc), lambda j: (j, 0, 0))],
        out_specs=[pl.BlockSpec((nb, t, LANES), lambda j: (0, 0, j)),
                   pl.BlockSpec((None, nb, ns2), lambda j: (j, 0, 0))],
        out_shape=[jax.ShapeDtypeStruct((nb, t, nt * LANES), F32), jax.ShapeDtypeStruct((nt, nb, ns2), F32)],
        scratch_shapes=[pltpu.VMEM((nb * tc, lc), F32), pltpu.VMEM((ns2 // LANES, nb * tc, LANES), F32),
                        pltpu.VMEM((ns2 // LANES, nb * tc, LANES), F32)],
        compiler_params=_cparams("arbitrary"),
        name="ssm_prompt",
    )(zrest3, h0cat, tmat, smat, omat, al, d_cat)


def _ssm_step_kernel(u_ref, h0_ref, s_ref, o_ref, a1_ref, d_ref, z_ref, h_ref, *, chunk):
    ns = a1_ref.shape[1] // 2
    u = u_ref[...]
    x = jnp.dot(u.astype(BF16), s_ref[(chunk - 1) * LANES:chunk * LANES, :], preferred_element_type=F32)
    ar = a1_ref[:, :ns]
    ai = a1_ref[:, ns:]
    h0 = h0_ref[...]
    hr = ar * h0[:, :ns] - ai * h0[:, ns:] + x[:, :ns]
    hi = ar * h0[:, ns:] + ai * h0[:, :ns] + x[:, ns:]
    h = jnp.concatenate([hr, hi], axis=1)
    h_ref[...] = h
    y = lax.dot_general(h.astype(BF16), o_ref[...], NT_DIMS, preferred_element_type=F32)
    z_ref[...] = jax.nn.gelu(y + d_ref[...] * u)


def ssm_step(zrest, col_u, wts, cmat, h0cat, d_t, chunk):
    _, smat, _, _, a1 = wts
    nb = zrest.shape[0]
    nt, lc, ns2 = smat.shape
    kern = functools.partial(_ssm_step_kernel, chunk=chunk)
    return pl.pallas_call(
        kern,
        grid=(nt,),
        in_specs=[pl.BlockSpec((nb, LANES), lambda j: (0, col_u // LANES + j)),
                  pl.BlockSpec((None, nb, ns2), lambda j: (j, 0, 0)),
                  pl.BlockSpec((None, lc, ns2), lambda j: (j, 0, 0)),
                  pl.BlockSpec((None, LANES, ns2), lambda j: (j, 0, 0)),
                  pl.BlockSpec((None, 1, ns2), lambda j: (j, 0, 0)),
                  pl.BlockSpec((None, 1, LANES), lambda j: (j, 0, 0))],
        out_specs=[pl.BlockSpec((nb, LANES), lambda j: (0, j)),
                   pl.BlockSpec((None, nb, ns2), lambda j: (j, 0, 0))],
        out_shape=[jax.ShapeDtypeStruct((nb, nt * LANES), F32), jax.ShapeDtypeStruct((nt, nb, ns2), F32)],
        compiler_params=_cparams("arbitrary"),
        name="ssm_step",
    )(zrest, h0cat, smat, cmat, a1, d_t)


def _glu_kernel(a_ref, zt_ref, w_ref, o_ref, wbf_ref):
    @pl.when(pl.program_id(1) == 0)
    def _():
        wbf_ref[...] = w_ref[...].astype(BF16)

    acc = jnp.dot(a_ref[...].astype(BF16), wbf_ref[...], preferred_element_type=F32)
    o_ref[...] = (zt_ref[...] * jax.nn.sigmoid(acc)).astype(o_ref.dtype)


def glu(z, w3, layer):
    m, k = z.shape
    bm = _pick(m, 1024)
    bn = _pick(k, 512)
    return pl.pallas_call(
        _glu_kernel,
        grid=(k // bn, m // bm),
        in_specs=[pl.BlockSpec((bm, k), lambda j, i: (i, 0)),
                  pl.BlockSpec((bm, bn), lambda j, i: (i, j)),
                  pl.BlockSpec((None, k, bn), lambda j, i: (layer, 0, j))],
        out_specs=pl.BlockSpec((bm, bn), lambda j, i: (i, j)),
        out_shape=jax.ShapeDtypeStruct((m, k), BF16),
        scratch_shapes=[pltpu.VMEM((k, bn), BF16)],
        compiler_params=_cparams("arbitrary", "arbitrary"),
        name="glu",
    )(z, z, w3)


def _merge_kernel(ya_ref, yc_ref, ys_ref, ga_ref, gc_ref, gs_ref, wa_ref, wc_ref, ws_ref, o_ref,
                  wa_bf, wc_bf, ws_bf):
    @pl.when(pl.program_id(1) == 0)
    def _():
        wa_bf[...] = wa_ref[...].astype(BF16)
        wc_bf[...] = wc_ref[...].astype(BF16)
        ws_bf[...] = ws_ref[...].astype(BF16)

    acc = jax.nn.sigmoid(ga_ref[...]) * jnp.dot(ya_ref[...], wa_bf[...], preferred_element_type=F32)
    acc = acc + jax.nn.sigmoid(gc_ref[...]) * jnp.dot(yc_ref[...], wc_bf[...], preferred_element_type=F32)
    acc = acc + jax.nn.sigmoid(gs_ref[...]) * jnp.dot(ys_ref[...], ws_bf[...], preferred_element_type=F32)
    o_ref[...] = acc.astype(o_ref.dtype)


def merge_branches(ya, yc, ys, zrest, col_g, w_a, w_c, w_s, layer, d):
    m = ya.shape[0]
    bm = _pick(m, 512)
    bn = _pick(d, 512)
    og = col_g // bn
    per = d // bn
    yspec = lambda y: pl.BlockSpec((bm, y.shape[1]), lambda j, i: (i, 0))
    gspec = lambda k: pl.BlockSpec((bm, bn), lambda j, i: (i, og + k * per + j))
    wspec = lambda w: pl.BlockSpec((None, w.shape[1], bn), lambda j, i: (layer, 0, j))
    return pl.pallas_call(
        _merge_kernel,
        grid=(per, m // bm),
        in_specs=[yspec(ya), yspec(yc), yspec(ys), gspec(0), gspec(1), gspec(2), wspec(w_a), wspec(w_c), wspec(w_s)],
        out_specs=pl.BlockSpec((bm, bn), lambda j, i: (i, j)),
        out_shape=jax.ShapeDtypeStruct((m, d), BF16),
        scratch_shapes=[pltpu.VMEM((w_a.shape[1], bn), BF16), pltpu.VMEM((w_c.shape[1], bn), BF16),
                        pltpu.VMEM((w_s.shape[1], bn), BF16)],
        compiler_params=_cparams("arbitrary", "arbitrary"),
        name="merge_branches",
    )(ya, yc, ys, zrest, zrest, zrest, w_a, w_c, w_s)


def _gather_lane_tiles(ref, lead=()):
    return jnp.concatenate([ref[lead + (slice(None), c, slice(None))] for c in range(ref.shape[-2])], axis=1)


def _scatter_lane_tiles(ref, val, lead=()):
    for c in range(ref.shape[-2]):
        ref[lead + (slice(None), c, slice(None))] = val[:, c * LANES:(c + 1) * LANES]


def _layer_norm(v, g, b):
    mu = jnp.mean(v, axis=-1, keepdims=True)
    var = jnp.mean(jnp.square(v - mu), axis=-1, keepdims=True)
    return (v - mu) * lax.rsqrt(var + LN_EPS) * g + b


def _ln_route_kernel(x_ref, y_ref, gate_ref, g_ref, b_ref, sc_ref, sh_ref, wr_ref, br_ref,
                     x1_ref, tok_ref, eidx_ref, wts_ref, *, alpha, n_groups, per_group):
    x1 = _layer_norm(alpha * x_ref[0] + (1.0 + gate_ref[0]) * y_ref[0], g_ref[...], b_ref[...])
    x1_ref[0] = x1
    tok = x1 * (1.0 + sc_ref[0]) + sh_ref[0]
    _scatter_lane_tiles(tok_ref, tok, (0,))
    logit = jnp.dot(tok.astype(BF16), wr_ref[...].astype(BF16), preferred_element_type=F32) + br_ref[...]
    lane_i = lax.broadcasted_iota(I32, logit.shape, 1)
    lane = lane_i.astype(F32)
    big = float(LANES)
    neg = -jnp.inf
    gl = jnp.where(lane < n_groups, logit, neg)
    gmax = jnp.max(gl, axis=1, keepdims=True)
    gidx = jnp.min(jnp.where(gl == gmax, lane, big), axis=1, keepdims=True)
    gprob = 1.0 / jnp.sum(jnp.exp(gl - gmax), axis=1, keepdims=True)
    lo = n_groups + gidx * per_group
    el = jnp.where((lane >= lo) & (lane < lo + per_group), logit, neg)
    t1 = jnp.max(el, axis=1, keepdims=True)
    i1 = jnp.min(jnp.where(el == t1, lane, big), axis=1, keepdims=True)
    el2 = jnp.where(lane == i1, neg, el)
    t2 = jnp.max(el2, axis=1, keepdims=True)
    i2 = jnp.min(jnp.where(el2 == t2, lane, big), axis=1, keepdims=True)
    e2 = jnp.exp(t2 - t1)
    w1 = gprob / (1.0 + e2)
    w2 = gprob * e2 / (1.0 + e2)
    eidx_ref[0] = jnp.where(lane_i == 0, i1 - n_groups, jnp.where(lane_i == 1, i2 - n_groups, 0.0)).astype(I32)
    wts_ref[0] = jnp.where(lane_i == 0, w1, jnp.where(lane_i == 1, w2, 0.0))


def ln_route(x3, y3, gate, ln_g, ln_b, sc, sh, w_router, b_router, alpha, n_groups, per_group):
    nb, t, d = x3.shape
    bt = _pick(t, 256)
    blk = pl.BlockSpec((1, bt, d), lambda b, i: (b, i, 0))
    row = pl.BlockSpec((1, d), lambda b, i: (0, 0))
    sel = pl.BlockSpec((1, bt, LANES), lambda b, i: (b, i, 0))
    tiles = pl.BlockSpec((1, bt, d // LANES, LANES), lambda b, i: (b, i, 0, 0))
    kern = functools.partial(_ln_route_kernel, alpha=alpha, n_groups=n_groups, per_group=per_group)
    return pl.pallas_call(
        kern,
        grid=(nb, t // bt),
        in_specs=[blk, blk, _mod_spec(gate, bt), row, row, _mod_spec(sc, bt), _mod_spec(sh, bt),
                  pl.BlockSpec((d, LANES), lambda b, i: (0, 0)), pl.BlockSpec((1, LANES), lambda b, i: (0, 0))],
        out_specs=[blk, tiles, sel, sel],
        out_shape=[jax.ShapeDtypeStruct((nb, t, d), F32), jax.ShapeDtypeStruct((nb, t, d // LANES, LANES), F32),
                   jax.ShapeDtypeStruct((nb, t, LANES), I32), jax.ShapeDtypeStruct((nb, t, LANES), F32)],
        compiler_params=_cparams("arbitrary", "arbitrary"),
        name="ln_route",
    )(x3, y3, gate, ln_g, ln_b, sc, sh, w_router, b_router)


def _ln_combine_kernel(x_ref, y0_ref, y1_ref, wts_ref, gate_ref, g_ref, b_ref, o_ref, *, alpha):
    w = wts_ref[0]
    ffn = w[:, 0:1] * _gather_lane_tiles(y0_ref, (0,)) + w[:, 1:2] * _gather_lane_tiles(y1_ref, (0,))
    o_ref[0] = _layer_norm(alpha * x_ref[0] + (1.0 + gate_ref[0]) * ffn, g_ref[...], b_ref[...])


def ln_combine(x3, y_assign, wts, gate, ln_g, ln_b, alpha):
    nb, t, d = x3.shape
    bt = _pick(t, 256)
    blk = pl.BlockSpec((1, bt, d), lambda b, i: (b, i, 0))
    row = pl.BlockSpec((1, d), lambda b, i: (0, 0))
    ysp = lambda k: pl.BlockSpec((None, 1, bt, d // LANES, LANES), lambda b, i: (k, b, i, 0, 0))
    kern = functools.partial(_ln_combine_kernel, alpha=alpha)
    return pl.pallas_call(
        kern,
        grid=(nb, t // bt),
        in_specs=[blk, ysp(0), ysp(1), pl.BlockSpec((1, bt, LANES), lambda b, i: (b, i, 0)),
                  _mod_spec(gate, bt), row, row],
        out_specs=blk,
        out_shape=jax.ShapeDtypeStruct((nb, t, d), F32),
        compiler_params=_cparams("arbitrary", "arbitrary"),
        name="ln_combine",
    )(x3, y_assign, y_assign, wts, gate, ln_g, ln_b)


def _rank_kernel(e_ref, rank_ref, cnt_ref, car_ref):
    first = (pl.program_id(0) == 0) & (pl.program_id(1) == 0)

    @pl.when(first)
    def _():
        car_ref[...] = jnp.zeros_like(car_ref)

    e = e_ref[...]
    rb = e.shape[1]
    ex = lax.broadcasted_iota(I32, (LANES, rb), 0)
    onehot = jnp.where(ex == e, 1.0, 0.0).astype(BF16)
    r = lax.broadcasted_iota(I32, (rb, rb), 0)
    c = lax.broadcasted_iota(I32, (rb, rb), 1)
    upto = jnp.where(r <= c, 1.0, 0.0).astype(BF16)
    cum = jnp.dot(onehot, upto, preferred_element_type=F32)
    oh = onehot.astype(F32)
    rank = jnp.sum(oh * (cum - 1.0 + car_ref[...]), axis=0, keepdims=True)
    rank_ref[...] = rank.astype(I32)
    car_ref[...] = car_ref[...] + jnp.sum(oh, axis=1, keepdims=True)
    cnt_ref[...] = jnp.broadcast_to(car_ref[...], cnt_ref.shape).astype(I32)


def expert_ranks(e_rows):
    two, nblk, _, rb = e_rows.shape
    return pl.pallas_call(
        _rank_kernel,
        grid=(two, nblk),
        in_specs=[pl.BlockSpec((None, None, 1, rb), lambda k, i: (k, i, 0, 0))],
        out_specs=[pl.BlockSpec((None, None, 1, rb), lambda k, i: (k, i, 0, 0)),
                   pl.BlockSpec((LANES, LANES), lambda k, i: (0, 0))],
        out_shape=[jax.ShapeDtypeStruct(e_rows.shape, I32), jax.ShapeDtypeStruct((LANES, LANES), I32)],
        scratch_shapes=[pltpu.VMEM((LANES, 1), F32)],
        compiler_params=_cparams("arbitrary", "arbitrary"),
        name="expert_ranks",
    )(e_rows)


def _expert_kernel(dst_ref, be_ref, nu_ref, tok_hbm, wg_ref, wu_ref, wd_ref, spill_in, y_hbm, spill_hbm,
                   xbuf, ybuf, gsem, ssem, *, rows, nblk, n_tok):
    del spill_in
    del be_ref
    i = pl.program_id(0)
    slot = lax.rem(i, 2)
    n_used = nu_ref[0]
    n_asg = 2 * n_tok

    def fetch(blk, sl):
        @pl.when(blk < n_used)
        def _():
            def body(u, c):
                a = dst_ref[blk * rows + u]
                pltpu.make_async_copy(tok_hbm.at[pl.ds(lax.rem(a, n_tok), 1)], xbuf.at[sl, pl.ds(u, 1)],
                                      gsem.at[sl]).start()
                return c

            lax.fori_loop(0, rows, body, 0, unroll=8)

    def store(blk, sl):
        @pl.when(blk < n_used)
        def _():
            def body(u, c):
                a = dst_ref[blk * rows + u]
                src = ybuf.at[sl, pl.ds(u, 1)]

                @pl.when(a < n_asg)
                def _():
                    pltpu.make_async_copy(src, y_hbm.at[pl.ds(a, 1)], ssem.at[sl]).start()

                @pl.when(a >= n_asg)
                def _():
                    pltpu.make_async_copy(src, spill_hbm.at[pl.ds(a - n_asg, 1)], ssem.at[sl]).start()

                return c

            lax.fori_loop(0, rows, body, 0, unroll=8)

    def wait_block(blk, buf, sem, sl):
        @pl.when(blk < n_used)
        def _():
            pltpu.make_async_copy(buf.at[sl], buf.at[sl], sem.at[sl]).wait()

    @pl.when(i == 0)
    def _():
        fetch(0, 0)

    @pl.when(i + 1 < nblk)
    def _():
        fetch(i + 1, 1 - slot)

    wait_block(i, xbuf, gsem, slot)

    @pl.when(i >= 2)
    def _():
        wait_block(i - 2, ybuf, ssem, slot)

    @pl.when(i < n_used)
    def _():
        xb = _gather_lane_tiles(xbuf.at[slot]).astype(BF16)
        g = jnp.dot(xb, wg_ref[...], preferred_element_type=F32)
        u = jnp.dot(xb, wu_ref[...], preferred_element_type=F32)
        h = (g * jax.nn.sigmoid(g) * u).astype(BF16)
        _scatter_lane_tiles(ybuf.at[slot], jnp.dot(h, wd_ref[...], preferred_element_type=F32))

    store(i, slot)

    @pl.when(i == nblk - 1)
    def _():
        if nblk >= 2:
            wait_block(i - 1, ybuf, ssem, 1 - slot)
        wait_block(i, ybuf, ssem, slot)


def expert_blocks(tok, dst_sorted, blk_e, n_used, wg, wu, wd, layer, rows):
    n, nc, _ = tok.shape
    d = nc * LANES
    hid = wg.shape[-1]
    nblk = dst_sorted.shape[0] // rows
    live = lambda i, nu: jnp.minimum(i, nu[0] - 1)
    hbm = pl.BlockSpec(memory_space=pl.ANY)
    kern = functools.partial(_expert_kernel, rows=rows, nblk=nblk, n_tok=n)
    spill = jnp.zeros((2 * rows, nc, LANES), F32)
    y, _ = pl.pallas_call(
        kern,
        grid_spec=pltpu.PrefetchScalarGridSpec(
            num_scalar_prefetch=3,
            grid=(nblk,),
            in_specs=[hbm,
                      pl.BlockSpec((None, None, d, hid), lambda i, ds, be, nu: (layer, be[live(i, nu)], 0, 0)),
                      pl.BlockSpec((None, None, d, hid), lambda i, ds, be, nu: (layer, be[live(i, nu)], 0, 0)),
                      pl.BlockSpec((None, None, hid, d), lambda i, ds, be, nu: (layer, be[live(i, nu)], 0, 0)),
                      hbm],
            out_specs=[hbm, hbm],
            scratch_shapes=[pltpu.VMEM((2, rows, nc, LANES), F32), pltpu.VMEM((2, rows, nc, LANES), F32),
                            pltpu.SemaphoreType.DMA((2,)), pltpu.SemaphoreType.DMA((2,))],
        ),
        out_shape=[jax.ShapeDtypeStruct((2 * n, nc, LANES), F32), jax.ShapeDtypeStruct(spill.shape, F32)],
        input_output_aliases={7: 1},
        compiler_params=_cparams("arbitrary"),
        name="expert_blocks",
    )(dst_sorted, blk_e, n_used, tok, wg, wu, wd, spill)
    return y


def hierarchical_moe(tok3, eidx, wg, wu, wd, layer, rows):
    nb, t, nc, _ = tok3.shape
    n = nb * t
    n_exp = wg.shape[1]
    e2 = eidx.reshape(n, LANES)[:, :2].T
    if n >= LANES:
        rb = _pick(n, 512)
        rank, cnt = expert_ranks(e2.reshape(2, n // rb, 1, rb))
        rank = rank.reshape(-1)
        counts = cnt[:n_exp, 0]
    else:
        ef = e2.reshape(-1)
        ar = jnp.arange(2 * n)
        rank = jnp.sum((ef[:, None] == ef[None, :]) & (ar[None, :] < ar[:, None]), axis=1).astype(I32)
        counts = jnp.sum(ef[:, None] == jnp.arange(n_exp)[None, :], axis=0).astype(I32)
    ef = e2.reshape(-1)
    nblk = min(-(-(2 * n + n_exp * (rows - 1)) // rows), 2 * n)
    padded = (counts + rows - 1) // rows * rows
    pad_end = jnp.cumsum(padded)
    dest = ((pad_end - padded)[ef] + rank).astype(I32)
    r = jnp.arange(nblk * rows, dtype=I32)
    spill_row = 2 * n + (r // rows) % 2 * rows + r % rows
    dst_sorted = spill_row.at[dest].set(jnp.arange(2 * n, dtype=I32))
    blk_e = jnp.minimum(jnp.sum(pad_end[None, :] <= (jnp.arange(nblk) * rows)[:, None], axis=1), n_exp - 1).astype(I32)
    n_used = (pad_end[-1:] // rows).astype(I32)
    y_assign = expert_blocks(tok3.reshape(n, nc, LANES), dst_sorted, blk_e, n_used, wg, wu, wd, layer, rows)
    return y_assign.reshape(2, nb, t, nc, LANES)


def _dims(p):
    d = p['w_o'].shape[-1]
    aw = p['w_br_attn'].shape[1]
    cw = p['w_br_conv'].shape[1]
    sw = p['w_br_ssm'].shape[1]
    return d, aw, cw, sw


def _mixer_common(xin, p, layer, nb, t, heads, cumsum):
    d, aw, cw, sw = _dims(p)
    wt = p['w_in_t']
    base = layer * (wt.shape[0] // p['w_o'].shape[0])
    zqkv = matmul_nt(xin, wt, base, 3 * aw)
    zrest = matmul_nt(xin, wt, base + 3 * aw + heads, 3 * cw + sw + 3 * d)
    b_pad = jnp.zeros((1, LANES), F32).at[0, :heads].set(p['b_forget'][layer])
    logf, fcum = forget_gate(xin, wt, base + 3 * aw, b_pad, nb, t, cumsum)
    return zqkv, zrest, logf, fcum


def _finish_mixer(x3, ya, yc, ys, zrest, p, layer, mods, alpha, n_groups, per_group, w_router, b_router):
    d, aw, cw, sw = _dims(p)
    nb, t, _ = x3.shape
    merged = merge_branches(ya, yc, ys, zrest, 3 * cw + sw, p['w_br_attn'], p['w_br_conv'], p['w_br_ssm'], layer, d)
    mix = matmul(merged, p['w_o'], layer)
    sh1, sc1, g1, sh2, sc2, g2 = mods
    return ln_route(x3, mix.reshape(nb, t, d), g1, p['ln1_g'][layer][None], p['ln1_b'][layer][None], sc2, sh2,
                    w_router, b_router, alpha, n_groups, per_group)


def _ssm_tiles(state, nt):
    nb = state.shape[0]
    return state.reshape(nb, nt, -1).transpose(1, 0, 2)


def _ssm_untile(h, nb, g, p):
    return h.transpose(1, 0, 2).reshape(nb, g, p)


def kernel(x_prompt, x_sample, c_prompt, c_sample, cache_k, cache_v, cache_logf, page_table, state_conv, state_ssm_re, state_ssm_im, w_ada, b_ada, ln1_g, ln1_b, ln2_g, ln2_b, w_in, b_forget, conv_w, ssm_lambda_re, ssm_lambda_im, ssm_log_dt, ssm_b_re, ssm_b_im, ssm_c_re, ssm_c_im, ssm_d, ssm_w_glu, w_br_attn, w_br_conv, w_br_ssm, w_o, router_w_group, router_b_group, router_w_expert, router_b_expert, moe_w_gate, moe_w_up, moe_w_down):
    p = dict(b_forget=b_forget, w_br_attn=w_br_attn, w_br_conv=w_br_conv, w_br_ssm=w_br_ssm, w_o=w_o,
             ln1_g=ln1_g, ln1_b=ln1_b, ln2_g=ln2_g, ln2_b=ln2_b)
    depth = w_in.shape[0]
    nbp, t, d = x_prompt.shape
    nbs, ts, _ = x_sample.shape
    assert ts == 1
    heads, dh = cache_k.shape[3], cache_k.shape[4]
    aw = heads * dh
    cw = conv_w.shape[-1]
    sw = ssm_d.shape[-1]
    g, pst = ssm_lambda_re.shape[1], ssm_lambda_re.shape[2]
    gc = ssm_b_re.shape[-1]
    nt = g * gc // LANES
    ns = (LANES // gc) * pst
    n_groups = router_w_group.shape[-1]
    n_exp = router_w_expert.shape[-1]
    per_group = n_exp // n_groups
    alpha = (2 * depth) ** 0.25
    chunk = SSM_CHUNK
    col_conv = 0
    col_u = 3 * cw

    r = nbp + nbs
    rpad = -(-r // 8) * 8
    c_all = jnp.zeros((rpad, d), F32).at[:nbp].set(c_prompt).at[nbp:r].set(c_sample)
    mod_all = ada_modulation(c_all, w_ada, b_ada)

    assert w_in.shape[2] % 8 == 0 and (3 * aw + heads) % 8 == 0
    p['w_in_t'] = jnp.swapaxes(w_in, 1, 2).reshape(depth * w_in.shape[2], d)

    wg_bf = moe_w_gate.astype(BF16)
    wu_bf = moe_w_up.astype(BF16)
    wd_bf = moe_w_down.astype(BF16)

    xp = x_prompt
    xs = x_sample.reshape(1, nbs, d)
    outs = {k: [] for k in ('kp', 'vp', 'fp', 'ks', 'vs', 'fs', 'cp', 'cs', 'srp', 'sip', 'srs', 'sis')}
    zero_conv = jnp.zeros((nbp, 2, cw), F32)
    zero_h = jnp.zeros((nt, nbp, 2 * ns), F32)
    for l in range(depth):
        mp = [mod_all[l, :nbp, i * d:(i + 1) * d].reshape(nbp, 1, d) for i in range(6)]
        ms = [mod_all[l, nbp:r, i * d:(i + 1) * d].reshape(1, nbs, d) for i in range(6)]
        w_router = jnp.zeros((d, LANES), F32).at[:, :n_groups].set(router_w_group[l]) \
            .at[:, n_groups:n_groups + n_exp].set(router_w_expert[l])
        b_router = jnp.zeros((1, LANES), F32).at[0, :n_groups].set(router_b_group[l]) \
            .at[0, n_groups:n_groups + n_exp].set(router_b_expert[l])
        wts = ssm_weights(ssm_lambda_re[l], ssm_lambda_im[l], ssm_log_dt[l], ssm_b_re[l], ssm_b_im[l],
                          ssm_c_re[l], ssm_c_im[l], chunk)
        d_t = ssm_d[l].reshape(nt, 1, LANES)
        d_cat = jnp.tile(d_t, (1, 1, chunk))
        eye = jnp.eye(LANES // gc, dtype=F32)
        cdiag = lambda c: jnp.einsum('tgcp,gh->thcgp', c.reshape(nt, LANES // gc, gc, pst), eye).reshape(nt, LANES, ns)
        cmat = jnp.concatenate([cdiag(ssm_c_re[l]), -cdiag(ssm_c_im[l])], axis=2).astype(BF16)

        n = nbp * t
        xin = modulate(xp, mp[1], mp[0]).reshape(n, d)
        zqkv, zrest, logf, fcum = _mixer_common(xin, p, l, nbp, t, heads, True)
        fcum_t = fcum[:, :heads].reshape(nbp, t, heads).transpose(0, 2, 1)
        ya = flash_attention(zqkv, fcum, fcum_t, nbp, t, heads, dh)
        yc, conv_p = short_conv_prompt(zrest, col_conv, conv_w[l], zero_conv, nbp, t, cw)
        zs, hfin = ssm_prompt(zrest.reshape(nbp, t, -1), col_u, wts, zero_h, d_cat, chunk)
        ys = glu(zs.reshape(n, sw), ssm_w_glu, l)
        x1, tok, eidx, wsel = _finish_mixer(xp, ya, yc, ys, zrest, p, l, mp, alpha, n_groups, per_group,
                                            w_router, b_router)
        y_assign = hierarchical_moe(tok, eidx, wg_bf, wu_bf, wd_bf, l, MOE_ROWS)
        xp = ln_combine(x1, y_assign, wsel, mp[5], ln2_g[l][None], ln2_b[l][None], alpha)
        outs['kp'].append(zqkv[:, aw:2 * aw].reshape(nbp, t, heads, dh))
        outs['vp'].append(zqkv[:, 2 * aw:].reshape(nbp, t, heads, dh))
        outs['fp'].append(logf[:, :heads].reshape(nbp, t, heads))
        outs['cp'].append(conv_p)
        outs['srp'].append(_ssm_untile(hfin[:, :, :ns], nbp, g, pst))
        outs['sip'].append(_ssm_untile(hfin[:, :, ns:], nbp, g, pst))

        xin_s = modulate(xs, ms[1], ms[0]).reshape(nbs, d)
        zqkv_s, zrest_s, logf_s, _ = _mixer_common(xin_s, p, l, 1, nbs, heads, False)
        q_s = zqkv_s[:, :aw].reshape(nbs, heads, dh)
        k_s = zqkv_s[:, aw:2 * aw].reshape(nbs, heads, dh)
        v_s = zqkv_s[:, 2 * aw:].reshape(nbs, heads, dh)
        ya_s = decode_attention(q_s, k_s, v_s, logf_s[:, :heads].reshape(nbs, 1, heads), cache_k, cache_v, cache_logf,
                                page_table, l)
        ya_s = ya_s.reshape(nbs, aw).astype(BF16)
        yc_s, v_row = short_conv_step(zrest_s, col_conv, conv_w[l], state_conv[l, :, 0], state_conv[l, :, 1], cw)
        h0 = jnp.concatenate([_ssm_tiles(state_ssm_re[l], nt), _ssm_tiles(state_ssm_im[l], nt)], axis=2)
        zs_s, h_s = ssm_step(zrest_s, col_u, wts, cmat, h0, d_t, chunk)
        ys_s = glu(zs_s, ssm_w_glu, l)
        x1_s, tok_s, eidx_s, wsel_s = _finish_mixer(xs, ya_s, yc_s, ys_s, zrest_s, p, l, ms, alpha, n_groups,
                                                    per_group, w_router, b_router)
        y_assign_s = hierarchical_moe(tok_s, eidx_s, wg_bf, wu_bf, wd_bf, l, 8)
        xs = ln_combine(x1_s, y_assign_s, wsel_s, ms[5], ln2_g[l][None], ln2_b[l][None], alpha)
        outs['ks'].append(k_s.reshape(nbs, 1, heads, dh))
        outs['vs'].append(v_s.reshape(nbs, 1, heads, dh))
        outs['fs'].append(logf_s[:, :heads].reshape(nbs, 1, heads))
        outs['cs'].append(jnp.stack([state_conv[l, :, 1], v_row], axis=1))
        outs['srs'].append(_ssm_untile(h_s[:, :, :ns], nbs, g, pst))
        outs['sis'].append(_ssm_untile(h_s[:, :, ns:], nbs, g, pst))

    st = lambda k: jnp.stack(outs[k])
    return (xp, xs.reshape(nbs, 1, d), st('kp'), st('vp'), st('fp'), st('ks'), st('vs'), st('fs'),
            st('cp'), st('cs'), st('srp'), st('sip'), st('srs'), st('sis'))
```

```python
import functools
import math

import jax
import jax.numpy as jnp
from jax import lax
from jax.experimental import pallas as pl
from jax.experimental.pallas import tpu as pltpu

F32 = jnp.float32
BF16 = jnp.bfloat16
I32 = jnp.int32

LANES = 128
VMEM_LIMIT = 56 * 1024 * 1024
LN_EPS = 1e-5
SSM_CHUNK = 8
MOE_ROWS = 128
NT_DIMS = (((1,), (1,)), ((), ()))


def _cparams(*sem):
    return pltpu.CompilerParams(dimension_semantics=sem, vmem_limit_bytes=VMEM_LIMIT)


def _pick(n, pref):
    if n <= pref:
        return n
    b = pref
    while n % b:
        b //= 2
    return b


def _ada_kernel(c_ref, w_ref, b_ref, o_ref):
    c = c_ref[...]
    s = (c * jax.nn.sigmoid(c)).astype(BF16)
    o_ref[...] = jnp.dot(s, w_ref[...].astype(BF16), preferred_element_type=F32) + b_ref[...]


def ada_modulation(c_all, w_ada, b_ada):
    depth, d, n6 = w_ada.shape
    r = c_all.shape[0]
    bn = _pick(n6, 512)
    return pl.pallas_call(
        _ada_kernel,
        grid=(depth, n6 // bn),
        in_specs=[
            pl.BlockSpec((r, d), lambda l, j: (0, 0)),
            pl.BlockSpec((None, d, bn), lambda l, j: (l, 0, j)),
            pl.BlockSpec((None, 1, bn), lambda l, j: (l, 0, j)),
        ],
        out_specs=pl.BlockSpec((None, r, bn), lambda l, j: (l, 0, j)),
        out_shape=jax.ShapeDtypeStruct((depth, r, n6), F32),
        compiler_params=_cparams("arbitrary", "arbitrary"),
        name="ada_modulation",
    )(c_all, w_ada, b_ada.reshape(depth, 1, n6))


def _mod_spec(mod, bt):
    if mod.shape[1] == 1:
        return pl.BlockSpec((1, 1, mod.shape[2]), lambda b, t: (b, 0, 0))
    return pl.BlockSpec((1, bt, mod.shape[2]), lambda b, t: (b, t, 0))


def _modulate_kernel(x_ref, sc_ref, sh_ref, o_ref):
    o_ref[...] = (x_ref[...] * (1.0 + sc_ref[...]) + sh_ref[...]).astype(o_ref.dtype)


def modulate(x3, sc, sh):
    nb, t, d = x3.shape
    bt = _pick(t, 512)
    return pl.pallas_call(
        _modulate_kernel,
        grid=(nb, t // bt),
        in_specs=[pl.BlockSpec((1, bt, d), lambda b, i: (b, i, 0)), _mod_spec(sc, bt), _mod_spec(sh, bt)],
        out_specs=pl.BlockSpec((1, bt, d), lambda b, i: (b, i, 0)),
        out_shape=jax.ShapeDtypeStruct((nb, t, d), BF16),
        compiler_params=_cparams("arbitrary", "arbitrary"),
        name="modulate",
    )(x3, sc, sh)


def _mm_kernel(a_ref, w_ref, o_ref, wbf_ref):
    @pl.when(pl.program_id(1) == 0)
    def _():
        wbf_ref[...] = w_ref[...].astype(BF16)

    o_ref[...] = jnp.dot(a_ref[...].astype(BF16), wbf_ref[...], preferred_element_type=F32).astype(o_ref.dtype)


def matmul(a, w3, layer, *, out_dtype=F32, bm=1024, bn=512):
    m, k = a.shape
    ncols = w3.shape[2]
    bm = _pick(m, bm)
    bn = _pick(ncols, bn)
    assert ncols % bn == 0 and m % bm == 0
    return pl.pallas_call(
        _mm_kernel,
        grid=(ncols // bn, m // bm),
        in_specs=[
            pl.BlockSpec((bm, k), lambda j, i: (i, 0)),
            pl.BlockSpec((None, k, bn), lambda j, i: (layer, 0, j)),
        ],
        out_specs=pl.BlockSpec((bm, bn), lambda j, i: (i, j)),
        out_shape=jax.ShapeDtypeStruct((m, ncols), out_dtype),
        scratch_shapes=[pltpu.VMEM((k, bn), BF16)],
        compiler_params=_cparams("arbitrary", "arbitrary"),
        name="matmul",
    )(a, w3)


def _mm_nt_kernel(a_ref, w_ref, o_ref, wbf_ref):
    @pl.when(pl.program_id(1) == 0)
    def _():
        wbf_ref[...] = w_ref[...].astype(BF16)

    o_ref[...] = lax.dot_general(a_ref[...].astype(BF16), wbf_ref[...], NT_DIMS,
                                 preferred_element_type=F32).astype(o_ref.dtype)


def _weight_rows(row0, bn, k):
    return pl.BlockSpec((pl.Element(bn), pl.Element(k)), lambda j, i: (pl.multiple_of(row0 + j * bn, 8), 0))


def _mm_nt2_kernel(a_ref, a2_ref, w_ref, o_ref, o2_ref, wbf_ref):
    @pl.when(pl.program_id(1) == 0)
    def _():
        wbf_ref[...] = w_ref[...].astype(BF16)
        o2_ref[...] = lax.dot_general(a2_ref[...].astype(BF16), wbf_ref[...], NT_DIMS,
                                      preferred_element_type=F32).astype(o2_ref.dtype)

    o_ref[...] = lax.dot_general(a_ref[...].astype(BF16), wbf_ref[...], NT_DIMS,
                                 preferred_element_type=F32).astype(o_ref.dtype)


def matmul_nt(a, wt, row0, ncols, *, a2=None, out_dtype=F32, bm=1024, bn=512):
    m, k = a.shape
    bm = _pick(m, bm)
    bn = _pick(ncols, bn)
    assert ncols % bn == 0 and m % bm == 0 and row0 % 8 == 0 and row0 + ncols <= wt.shape[0]
    grid = (ncols // bn, m // bm)
    a_spec = pl.BlockSpec((bm, k), lambda j, i: (i, 0))
    o_spec = pl.BlockSpec((bm, bn), lambda j, i: (i, j))
    scratch = [pltpu.VMEM((bn, k), BF16)]
    if a2 is None:
        return pl.pallas_call(
            _mm_nt_kernel,
            grid=grid,
            in_specs=[a_spec, _weight_rows(row0, bn, k)],
            out_specs=o_spec,
            out_shape=jax.ShapeDtypeStruct((m, ncols), out_dtype),
            scratch_shapes=scratch,
            compiler_params=_cparams("arbitrary", "arbitrary"),
            name="matmul_nt",
        )(a, wt)
    m2 = a2.shape[0]
    return pl.pallas_call(
        _mm_nt2_kernel,
        grid=grid,
        in_specs=[a_spec, pl.BlockSpec((m2, k), lambda j, i: (0, 0)), _weight_rows(row0, bn, k)],
        out_specs=[o_spec, pl.BlockSpec((m2, bn), lambda j, i: (0, j))],
        out_shape=[jax.ShapeDtypeStruct((m, ncols), out_dtype), jax.ShapeDtypeStruct((m2, ncols), out_dtype)],
        scratch_shapes=scratch,
        compiler_params=_cparams("arbitrary", "arbitrary"),
        name="matmul_nt",
    )(a, a2, wt)


def _forget_kernel(a_ref, w_ref, b_ref, lf_ref, fc_ref, carry_ref, *, cumsum):
    t = pl.program_id(1)
    z = lax.dot_general(a_ref[...].astype(BF16), w_ref[...].astype(BF16), NT_DIMS,
                        preferred_element_type=F32) + b_ref[...]
    lf = jnp.minimum(z, 0.0) - jnp.log1p(jnp.exp(-jnp.abs(z)))
    lf_ref[...] = lf
    if not cumsum:
        fc_ref[...] = lf
        return

    @pl.when(t == 0)
    def _():
        carry_ref[...] = jnp.zeros_like(carry_ref)

    bt = lf.shape[0]
    row = lax.broadcasted_iota(I32, lf.shape, 0)
    acc = lf
    s = 1
    while s < bt:
        acc = acc + jnp.where(row >= s, pltpu.roll(acc, s, axis=0), 0.0)
        s *= 2
    acc = acc + carry_ref[...]
    fc_ref[...] = acc
    carry_ref[...] = acc[bt - 1:bt, :]


def forget_gate(xin, wt, row0, b_pad, nb, t, cumsum):
    n, d = xin.shape
    bt = _pick(t, 512)
    nt = t // bt
    assert row0 % 8 == 0 and row0 + LANES <= wt.shape[0]
    kern = functools.partial(_forget_kernel, cumsum=cumsum)
    return pl.pallas_call(
        kern,
        grid=(nb, nt),
        in_specs=[
            pl.BlockSpec((bt, d), lambda b, i: (b * nt + i, 0)),
            pl.BlockSpec((pl.Element(LANES), pl.Element(d)), lambda b, i: (row0, 0)),
            pl.BlockSpec((1, LANES), lambda b, i: (0, 0)),
        ],
        out_specs=[pl.BlockSpec((bt, LANES), lambda b, i: (b * nt + i, 0))] * 2,
        out_shape=[jax.ShapeDtypeStruct((n, LANES), F32)] * 2,
        scratch_shapes=[pltpu.VMEM((1, LANES), F32)],
        compiler_params=_cparams("arbitrary", "arbitrary"),
        name="forget_gate",
    )(xin, wt, b_pad)


def _flash_kernel(qi_ref, ki_ref, q_ref, k_ref, v_ref, fq_ref, fk_ref, o_ref, m_ref, l_ref, acc_ref, fqc_ref,
                  *, hp, dh):
    hb = pl.program_id(1) * hp
    step = pl.program_id(2)
    qi = qi_ref[step]
    ki = ki_ref[step]

    @pl.when(ki == 0)
    def _():
        m_ref[...] = jnp.full_like(m_ref, -jnp.inf)
        l_ref[...] = jnp.zeros_like(l_ref)
        acc_ref[...] = jnp.zeros_like(acc_ref)
        lane = lax.broadcasted_iota(I32, fq_ref.shape, 1)
        for hh in range(hp):
            fqc_ref[hh] = jnp.sum(jnp.where(lane == hb + hh, fq_ref[...], 0.0), axis=1, keepdims=True)

    def block(masked):
        for hh in range(hp):
            cols = slice(hh * dh, (hh + 1) * dh)
            s = lax.dot_general(q_ref[:, cols].astype(BF16), k_ref[:, cols].astype(BF16), NT_DIMS,
                                preferred_element_type=F32) * (dh ** -0.5)
            s = s + fqc_ref[hh] - fk_ref[pl.ds(hb + hh, 1), :]
            if masked:
                row = lax.broadcasted_iota(I32, s.shape, 0)
                col = lax.broadcasted_iota(I32, s.shape, 1)
                s = jnp.where(col <= row, s, -jnp.inf)
            m_prev = m_ref[hh]
            m_new = jnp.maximum(m_prev, jnp.max(s, axis=1, keepdims=True))
            alpha = jnp.exp(m_prev - m_new)
            p = jnp.exp(s - m_new)
            l_ref[hh] = alpha * l_ref[hh] + jnp.sum(p, axis=1, keepdims=True)
            acc_ref[hh] = alpha * acc_ref[hh] + jnp.dot(p.astype(BF16), v_ref[:, cols].astype(BF16),
                                                        preferred_element_type=F32)
            m_ref[hh] = m_new

    @pl.when(ki < qi)
    def _():
        block(False)

    @pl.when(ki == qi)
    def _():
        block(True)
        for hh in range(hp):
            o_ref[:, hh * dh:(hh + 1) * dh] = (acc_ref[hh] / l_ref[hh]).astype(o_ref.dtype)


def flash_attention(zqkv, fcum, fcum_t, nb, t, heads, dh):
    n = nb * t
    bq = _pick(t, 512)
    nq = t // bq
    hp = 2 if heads % 2 == 0 else 1
    ng = heads // hp
    pairs = [(i, j) for i in range(nq) for j in range(i + 1)]
    qi_tab = jnp.asarray([p[0] for p in pairs], I32)
    ki_tab = jnp.asarray([p[1] for p in pairs], I32)
    kern = functools.partial(_flash_kernel, hp=hp, dh=dh)
    w = hp * dh
    return pl.pallas_call(
        kern,
        grid_spec=pltpu.PrefetchScalarGridSpec(
            num_scalar_prefetch=2,
            grid=(nb, ng, len(pairs)),
            in_specs=[
                pl.BlockSpec((bq, w), lambda b, g, s, qt, kt: (b * nq + qt[s], g)),
                pl.BlockSpec((bq, w), lambda b, g, s, qt, kt: (b * nq + kt[s], ng + g)),
                pl.BlockSpec((bq, w), lambda b, g, s, qt, kt: (b * nq + kt[s], 2 * ng + g)),
                pl.BlockSpec((bq, LANES), lambda b, g, s, qt, kt: (b * nq + qt[s], 0)),
                pl.BlockSpec((None, heads, bq), lambda b, g, s, qt, kt: (b, 0, kt[s])),
            ],
            out_specs=pl.BlockSpec((bq, w), lambda b, g, s, qt, kt: (b * nq + qt[s], g)),
            scratch_shapes=[pltpu.VMEM((hp, bq, 1), F32), pltpu.VMEM((hp, bq, 1), F32),
                            pltpu.VMEM((hp, bq, dh), F32), pltpu.VMEM((hp, bq, 1), F32)],
        ),
        out_shape=jax.ShapeDtypeStruct((n, heads * dh), BF16),
        compiler_params=_cparams("arbitrary", "arbitrary", "arbitrary"),
        name="flash_attention",
    )(qi_tab, ki_tab, zqkv, zqkv, zqkv, fcum, fcum_t)


def _decode_kernel(pt_ref, q_ref, kn_ref, vn_ref, ln_ref, lp_hbm, k_hbm, v_hbm, o_ref,
                   lp_buf, k_buf, v_buf, sem, m_ref, l_ref, acc_ref, car_ref,
                   *, layer, scale, n_seq, n_pages, group):
    b = pl.program_id(0)
    i = pl.program_id(1)
    n_steps = n_pages // group
    step = b * n_steps + i
    slot = lax.rem(step, 2)

    def page_copies(seq, st, sl):
        out = []
        for g in range(group):
            pg = pt_ref[seq * n_pages + (n_pages - 1 - (st * group + g))]
            out.append(pltpu.make_async_copy(lp_hbm.at[layer, pg], lp_buf.at[sl, g], sem.at[sl]))
            out.append(pltpu.make_async_copy(k_hbm.at[layer, pg], k_buf.at[sl, g], sem.at[sl]))
            out.append(pltpu.make_async_copy(v_hbm.at[layer, pg], v_buf.at[sl, g], sem.at[sl]))
        return out

    @pl.when(step == 0)
    def _():
        for cp in page_copies(b, i, slot):
            cp.start()

    @pl.when(step + 1 < n_seq * n_steps)
    def _():
        wrap = i + 1 == n_steps
        for cp in page_copies(jnp.where(wrap, b + 1, b), jnp.where(wrap, 0, i + 1), 1 - slot):
            cp.start()

    for cp in page_copies(b, i, slot):
        cp.wait()
    lp_refs = [lp_buf.at[slot, g] for g in range(group)]
    k_refs = [k_buf.at[slot, g] for g in range(group)]
    v_refs = [v_buf.at[slot, g] for g in range(group)]

    @pl.when(i == 0)
    def _():
        m_ref[...] = jnp.full_like(m_ref, -jnp.inf)
        l_ref[...] = jnp.zeros_like(l_ref)
        acc_ref[...] = jnp.zeros_like(acc_ref)
        car_ref[...] = ln_ref[...]

    q = q_ref[...]
    qs = q * scale
    page = lp_refs[0].shape[0]
    r = lax.broadcasted_iota(I32, (page, page), 0)
    c = lax.broadcasted_iota(I32, (page, page), 1)
    after = jnp.where(c > r, 1.0, 0.0).astype(F32)
    carry = car_ref[...]
    scores = []
    for g in range(group):
        lp = lp_refs[g][...]
        suffix = jnp.dot(after, lp, preferred_element_type=F32, precision=lax.Precision.HIGHEST) + carry
        carry = suffix[0:1, :] + lp[0:1, :]
        scores.append(jnp.sum(k_refs[g][...] * qs[None], axis=-1, keepdims=True) + suffix[:, :, None])
    car_ref[...] = carry
    m_prev = m_ref[...]
    m_new = m_prev
    for s in scores:
        m_new = jnp.maximum(m_new, jnp.max(s, axis=0))
    alpha = jnp.exp(m_prev - m_new)
    l_new = alpha * l_ref[...]
    acc = alpha * acc_ref[...]
    for g in range(group):
        p = jnp.exp(scores[g] - m_new[None])
        l_new = l_new + jnp.sum(p, axis=0)
        acc = acc + jnp.sum(p * v_refs[g][...], axis=0)
    l_ref[...] = l_new
    acc_ref[...] = acc
    m_ref[...] = m_new

    @pl.when(i == n_steps - 1)
    def _():
        s_new = jnp.sum(qs * kn_ref[...], axis=-1, keepdims=True)
        m_fin = jnp.maximum(m_new, s_new)
        a = jnp.exp(m_new - m_fin)
        p_new = jnp.exp(s_new - m_fin)
        o_ref[...] = (a * acc + p_new * vn_ref[...]) / (a * l_new + p_new)


def decode_attention(q, k_new, v_new, logf_new, cache_k, cache_v, cache_logf, page_table, layer):
    nb, heads, dh = q.shape
    page = cache_k.shape[2]
    n_pages = page_table.shape[1]
    group = _pick(n_pages, 4)
    n_steps = n_pages // group
    pt = page_table.reshape(-1).astype(I32)
    kern = functools.partial(_decode_kernel, layer=layer, scale=dh ** -0.5, n_seq=nb, n_pages=n_pages, group=group)
    tok = pl.BlockSpec((None, heads, dh), lambda b, i, pt_ref: (b, 0, 0))
    hbm = pl.BlockSpec(memory_space=pl.ANY)
    return pl.pallas_call(
        kern,
        grid_spec=pltpu.PrefetchScalarGridSpec(
            num_scalar_prefetch=1,
            grid=(nb, n_steps),
            in_specs=[tok, tok, tok, pl.BlockSpec((None, 1, heads), lambda b, i, pt_ref: (b, 0, 0)), hbm, hbm, hbm],
            out_specs=tok,
            scratch_shapes=[pltpu.VMEM((2, group, page, heads), F32),
                            pltpu.VMEM((2, group, page, heads, dh), F32),
                            pltpu.VMEM((2, group, page, heads, dh), F32),
                            pltpu.SemaphoreType.DMA((2,)),
                            pltpu.VMEM((heads, 1), F32), pltpu.VMEM((heads, 1), F32), pltpu.VMEM((heads, dh), F32),
                            pltpu.VMEM((1, heads), F32)],
        ),
        out_shape=jax.ShapeDtypeStruct((nb, heads, dh), F32),
        compiler_params=_cparams("arbitrary", "arbitrary"),
        name="decode_attention",
    )(pt, q, k_new, v_new, logf_new, cache_logf, cache_k, cache_v)


def _conv_kernel(b_ref, c_ref, x_ref, w_ref, prev_ref, y_ref, st_ref, car_ref):
    t = pl.program_id(2)
    v = c_ref[...] * x_ref[...]
    bt = v.shape[0]

    @pl.when(t == 0)
    def _():
        car_ref[...] = prev_ref[0]

    p0 = car_ref[0:1, :]
    p1 = car_ref[1:2, :]
    row = lax.broadcasted_iota(I32, v.shape, 0)
    r1 = jnp.where(row == 0, p1, pltpu.roll(v, 1, axis=0))
    r2 = jnp.where(row == 0, p0, jnp.where(row == 1, p1, pltpu.roll(v, 2, axis=0)))
    w = w_ref[...]
    y = w[0:1, :] * r2 + w[1:2, :] * r1 + w[2:3, :] * v
    y_ref[...] = (b_ref[...] * y).astype(y_ref.dtype)
    tail = v[bt - 2:bt, :]
    car_ref[...] = tail
    st_ref[0] = tail


def short_conv_prompt(zrest, col_b, conv_w_l, prev, nb, t, cw):
    n = nb * t
    bt = _pick(t, 512)
    bc = _pick(cw, 512)
    nt = t // bt
    o = col_b // bc
    per = cw // bc
    zspec = lambda k: pl.BlockSpec((bt, bc), lambda b, c, i: (b * nt + i, o + k * per + c))
    return pl.pallas_call(
        _conv_kernel,
        grid=(nb, per, nt),
        in_specs=[zspec(0), zspec(1), zspec(2),
                  pl.BlockSpec((3, bc), lambda b, c, i: (0, c)),
                  pl.BlockSpec((1, 2, bc), lambda b, c, i: (b, 0, c))],
        out_specs=[pl.BlockSpec((bt, bc), lambda b, c, i: (b * nt + i, c)),
                   pl.BlockSpec((1, 2, bc), lambda b, c, i: (b, 0, c))],
        out_shape=[jax.ShapeDtypeStruct((n, cw), BF16), jax.ShapeDtypeStruct((nb, 2, cw), F32)],
        scratch_shapes=[pltpu.VMEM((2, bc), F32)],
        compiler_params=_cparams("arbitrary", "arbitrary", "arbitrary"),
        name="short_conv_prompt",
    )(zrest, zrest, zrest, conv_w_l, prev)


def _conv_step_kernel(b_ref, c_ref, x_ref, w_ref, p0_ref, p1_ref, y_ref, v_ref):
    v = c_ref[...] * x_ref[...]
    w = w_ref[...]
    y = w[0:1, :] * p0_ref[...] + w[1:2, :] * p1_ref[...] + w[2:3, :] * v
    y_ref[...] = (b_ref[...] * y).astype(y_ref.dtype)
    v_ref[...] = v


def short_conv_step(zrest, col_b, conv_w_l, prev0, prev1, cw):
    nb = zrest.shape[0]
    bc = _pick(cw, 512)
    o = col_b // bc
    per = cw // bc
    zspec = lambda k: pl.BlockSpec((nb, bc), lambda c: (0, o + k * per + c))
    vec = pl.BlockSpec((nb, bc), lambda c: (0, c))
    return pl.pallas_call(
        _conv_step_kernel,
        grid=(per,),
        in_specs=[zspec(0), zspec(1), zspec(2), pl.BlockSpec((3, bc), lambda c: (0, c)), vec, vec],
        out_specs=[vec, vec],
        out_shape=[jax.ShapeDtypeStruct((nb, cw), BF16), jax.ShapeDtypeStruct((nb, cw), F32)],
        compiler_params=_cparams("arbitrary"),
        name="short_conv_step",
    )(zrest, zrest, zrest, conv_w_l, prev0, prev1)


def _ssm_weights_kernel(lre_ref, lim_ref, ldt_ref, bre_ref, bim_ref, cre_ref, cim_ref,
                        t_ref, s_ref, o_ref, al_ref, a1_ref, *, chunk):
    ns = lre_ref.shape[1]
    lr = lre_ref[...]
    li = lim_ref[...]
    dt = jnp.exp(ldt_ref[...])
    mag = jnp.exp(lr * dt)
    ang = li * dt
    ar = mag * jnp.cos(ang)
    ai = mag * jnp.sin(ang)
    den = lr * lr + li * li
    nr = ar - 1.0
    fr = (nr * lr + ai * li) / den
    fi = (ai * lr - nr * li) / den
    bre = bre_ref[...]
    bim = bim_ref[...]
    bbr = fr * bre - fi * bim
    bbi = fr * bim + fi * bre
    cre = cre_ref[...]
    cim = cim_ref[...]
    cfull = jnp.concatenate([cre, -cim], axis=1)
    pows = []
    pr = jnp.ones_like(ar)
    pi = jnp.zeros_like(ar)
    for _ in range(chunk + 1):
        pows.append((pr, pi))
        pr, pi = pr * ar - pi * ai, pr * ai + pi * ar
    a1_ref[...] = jnp.concatenate([ar, ai], axis=1)
    al_ref[...] = jnp.concatenate(list(pows[chunk]), axis=1)
    taps = [None] * chunk
    for j in range(chunk):
        qr, qi = pows[chunk - 1 - j]
        blk = jnp.concatenate([qr * bbr - qi * bbi, qr * bbi + qi * bbr], axis=1)
        s_ref[j * LANES:(j + 1) * LANES, :] = blk.astype(s_ref.dtype)
        taps[chunk - 1 - j] = lax.dot_general(blk, cfull, NT_DIMS, preferred_element_type=F32,
                                              precision=lax.Precision.HIGHEST)
    zero = jnp.zeros((LANES, LANES), F32)
    for j in range(chunk):
        for i in range(chunk):
            t_ref[j * LANES:(j + 1) * LANES, i * LANES:(i + 1) * LANES] = (
                taps[i - j] if i >= j else zero).astype(t_ref.dtype)
    for i in range(chunk):
        qr, qi = pows[i + 1]
        o_ref[i * LANES:(i + 1) * LANES, :] = jnp.concatenate(
            [cre * qr - cim * qi, -cre * qi - cim * qr], axis=1).astype(o_ref.dtype)


def ssm_weights(lam_re, lam_im, log_dt, b_re, b_im, c_re, c_im, chunk):
    g, p = lam_re.shape
    gc = b_re.shape[-1]
    gpt = LANES // gc
    nt = g // gpt
    ns = gpt * p
    eye = jnp.eye(gpt, dtype=F32)
    tile = lambda x: x.reshape(nt, 1, ns)
    ldt = tile(jnp.broadcast_to(log_dt[:, None], (g, p)))
    bdiag = lambda b: jnp.einsum('tgpc,gh->thcgp', b.reshape(nt, gpt, p, gc), eye).reshape(nt, LANES, ns)
    cdiag = lambda c: jnp.einsum('tgcp,gh->thcgp', c.reshape(nt, gpt, gc, p), eye).reshape(nt, LANES, ns)
    row = pl.BlockSpec((None, 1, ns), lambda j: (j, 0, 0))
    mat = pl.BlockSpec((None, LANES, ns), lambda j: (j, 0, 0))
    lc = chunk * LANES
    kern = functools.partial(_ssm_weights_kernel, chunk=chunk)
    return pl.pallas_call(
        kern,
        grid=(nt,),
        in_specs=[row, row, row, mat, mat, mat, mat],
        out_specs=[pl.BlockSpec((None, lc, lc), lambda j: (j, 0, 0)),
                   pl.BlockSpec((None, lc, 2 * ns), lambda j: (j, 0, 0)),
                   pl.BlockSpec((None, lc, 2 * ns), lambda j: (j, 0, 0)),
                   pl.BlockSpec((None, 1, 2 * ns), lambda j: (j, 0, 0)),
                   pl.BlockSpec((None, 1, 2 * ns), lambda j: (j, 0, 0))],
        out_shape=[jax.ShapeDtypeStruct((nt, lc, lc), BF16),
                   jax.ShapeDtypeStruct((nt, lc, 2 * ns), BF16),
                   jax.ShapeDtypeStruct((nt, lc, 2 * ns), BF16),
                   jax.ShapeDtypeStruct((nt, 1, 2 * ns), F32),
                   jax.ShapeDtypeStruct((nt, 1, 2 * ns), F32)],
        compiler_params=_cparams("arbitrary"),
        name="ssm_weights",
    )(tile(lam_re), tile(lam_im), ldt, bdiag(b_re), bdiag(b_im), cdiag(c_re), cdiag(c_im))


def _ssm_kernel(u_ref, h0_ref, t_ref, s_ref, o_ref, al_ref, d_ref, z_ref, hf_ref, ucat_ref, sloc_ref, sprev_ref,
                *, chunk):
    nb, t, _ = u_ref.shape
    tc = t // chunk
    ns = al_ref.shape[1] // 2
    for b in range(nb):
        for j in range(chunk):
            ucat_ref[b * tc:(b + 1) * tc, j * LANES:(j + 1) * LANES] = u_ref[b, pl.ds(j, tc, stride=chunk), :]
    nc = ns // LANES
    for b in range(nb):
        rows = slice(b * tc, (b + 1) * tc)
        sl = jnp.dot(ucat_ref[rows, :].astype(BF16), s_ref[...], preferred_element_type=F32)
        for c in range(2 * nc):
            sloc_ref[c, rows, :] = sl[:, c * LANES:(c + 1) * LANES]

    def step(k, h):
        at = pl.ds(k, nb, stride=tc)
        new = [None] * (2 * nc)
        for c in range(nc):
            lanes = slice(c * LANES, (c + 1) * LANES)
            alr = al_ref[:, lanes]
            ali = al_ref[:, ns + c * LANES:ns + (c + 1) * LANES]
            hr = h[c]
            hi = h[nc + c]
            sprev_ref[c, at, :] = hr
            sprev_ref[nc + c, at, :] = hi
            new[c] = alr * hr - ali * hi + sloc_ref[c, at, :]
            new[nc + c] = alr * hi + ali * hr + sloc_ref[nc + c, at, :]
        return tuple(new)

    h0 = h0_ref[...]
    hfin = lax.fori_loop(0, tc, step, tuple(h0[:, c * LANES:(c + 1) * LANES] for c in range(2 * nc)))
    hf_ref[...] = jnp.concatenate(hfin, axis=1)
    for b in range(nb):
        rows = slice(b * tc, (b + 1) * tc)
        uc = ucat_ref[rows, :]
        sp = jnp.concatenate([sprev_ref[c, rows, :] for c in range(2 * nc)], axis=1)
        y = jnp.dot(uc.astype(BF16), t_ref[...], preferred_element_type=F32)
        y = y + lax.dot_general(sp.astype(BF16), o_ref[...], NT_DIMS, preferred_element_type=F32)
        z = jax.nn.gelu(y + d_ref[...] * uc)
        for i in range(chunk):
            z_ref[b, pl.ds(i, tc, stride=chunk), :] = z[:, i * LANES:(i + 1) * LANES]


def ssm_prompt(zrest3, col_u, wts, h0cat, d_cat, chunk):
    tmat, smat, omat, al, _ = wts
    nb, t, _ = zrest3.shape
    nt, lc, ns2 = smat.shape
    tc = t // chunk
    kern = functools.partial(_ssm_kernel, chunk=chunk)
    return pl.pallas_call(
        kern,
        grid=(nt,),
        in_specs=[pl.BlockSpec((nb, t, LANES), lambda j: (0, 0, col_u // LANES + j)),
                  pl.BlockSpec((None, nb, ns2), lambda j: (j, 0, 0)),
                  pl.BlockSpec((None, lc, lc), lambda j: (j, 0, 0)),
                  pl.BlockSpec((None, lc, ns2), lambda j: (j, 0, 0)),
                  pl.BlockSpec((None, lc, ns2), lambda j: (j, 0, 0)),
                  pl.BlockSpec((None, 1, ns2), lambda j: (j, 0, 0)),
                  pl.BlockSpec((None, 1, lc), lambda j: (j, 0, 0))],
        out_specs=[pl.BlockSpec((nb, t, LANES), lambda j: (0, 0, j)),
                   pl.BlockSpec((None, nb, ns2), lambda j: (j, 0, 0))],
        out_shape=[jax.ShapeDtypeStruct((nb, t, nt * LANES), F32), jax.ShapeDtypeStruct((nt, nb, ns2), F32)],
        scratch_shapes=[pltpu.VMEM((nb * tc, lc), F32), pltpu.VMEM((ns2 // LANES, nb * tc, LANES), F32),
                        pltpu.VMEM((ns2 // LANES, nb * tc, LANES), F32)],
        compiler_params=_cparams("arbitrary"),
        name="ssm_prompt",
    )(zrest3, h0cat, tmat, smat, omat, al, d_cat)


def _ssm_step_kernel(u_ref, h0_ref, s_ref, o_ref, a1_ref, d_ref, z_ref, h_ref, *, chunk):
    ns = a1_ref.shape[1] // 2
    u = u_ref[...]
    x = jnp.dot(u.astype(BF16), s_ref[(chunk - 1) * LANES:chunk * LANES, :], preferred_element_type=F32)
    ar = a1_ref[:, :ns]
    ai = a1_ref[:, ns:]
    h0 = h0_ref[...]
    hr = ar * h0[:, :ns] - ai * h0[:, ns:] + x[:, :ns]
    hi = ar * h0[:, ns:] + ai * h0[:, :ns] + x[:, ns:]
    h = jnp.concatenate([hr, hi], axis=1)
    h_ref[...] = h
    y = lax.dot_general(h.astype(BF16), o_ref[...], NT_DIMS, preferred_element_type=F32)
    z_ref[...] = jax.nn.gelu(y + d_ref[...] * u)


def ssm_step(zrest, col_u, wts, cmat, h0cat, d_t, chunk):
    _, smat, _, _, a1 = wts
    nb = zrest.shape[0]
    nt, lc, ns2 = smat.shape
    kern = functools.partial(_ssm_step_kernel, chunk=chunk)
    return pl.pallas_call(
        kern,
        grid=(nt,),
        in_specs=[pl.BlockSpec((nb, LANES), lambda j: (0, col_u // LANES + j)),
                  pl.BlockSpec((None, nb, ns2), lambda j: (j, 0, 0)),
                  pl.BlockSpec((None, lc, ns2), lambda j: (j, 0, 0)),
                  pl.BlockSpec((None, LANES, ns2), lambda j: (j, 0, 0)),
                  pl.BlockSpec((None, 1, ns2), lambda j: (j, 0, 0)),
                  pl.BlockSpec((None, 1, LANES), lambda j: (j, 0, 0))],
        out_specs=[pl.BlockSpec((nb, LANES), lambda j: (0, j)),
                   pl.BlockSpec((None, nb, ns2), lambda j: (j, 0, 0))],
        out_shape=[jax.ShapeDtypeStruct((nb, nt * LANES), F32), jax.ShapeDtypeStruct((nt, nb, ns2), F32)],
        compiler_params=_cparams("arbitrary"),
        name="ssm_step",
    )(zrest, h0cat, smat, cmat, a1, d_t)


def _glu_kernel(a_ref, zt_ref, w_ref, o_ref, wbf_ref):
    @pl.when(pl.program_id(1) == 0)
    def _():
        wbf_ref[...] = w_ref[...].astype(BF16)

    acc = jnp.dot(a_ref[...].astype(BF16), wbf_ref[...], preferred_element_type=F32)
    o_ref[...] = (zt_ref[...] * jax.nn.sigmoid(acc)).astype(o_ref.dtype)


def glu(z, w3, layer):
    m, k = z.shape
    bm = _pick(m, 1024)
    bn = _pick(k, 512)
    return pl.pallas_call(
        _glu_kernel,
        grid=(k // bn, m // bm),
        in_specs=[pl.BlockSpec((bm, k), lambda j, i: (i, 0)),
                  pl.BlockSpec((bm, bn), lambda j, i: (i, j)),
                  pl.BlockSpec((None, k, bn), lambda j, i: (layer, 0, j))],
        out_specs=pl.BlockSpec((bm, bn), lambda j, i: (i, j)),
        out_shape=jax.ShapeDtypeStruct((m, k), BF16),
        scratch_shapes=[pltpu.VMEM((k, bn), BF16)],
        compiler_params=_cparams("arbitrary", "arbitrary"),
        name="glu",
    )(z, z, w3)


def _merge_kernel(ya_ref, yc_ref, ys_ref, ga_ref, gc_ref, gs_ref, wa_ref, wc_ref, ws_ref, o_ref,
                  wa_bf, wc_bf, ws_bf):
    @pl.when(pl.program_id(1) == 0)
    def _():
        wa_bf[...] = wa_ref[...].astype(BF16)
        wc_bf[...] = wc_ref[...].astype(BF16)
        ws_bf[...] = ws_ref[...].astype(BF16)

    acc = jax.nn.sigmoid(ga_ref[...]) * jnp.dot(ya_ref[...], wa_bf[...], preferred_element_type=F32)
    acc = acc + jax.nn.sigmoid(gc_ref[...]) * jnp.dot(yc_ref[...], wc_bf[...], preferred_element_type=F32)
    acc = acc + jax.nn.sigmoid(gs_ref[...]) * jnp.dot(ys_ref[...], ws_bf[...], preferred_element_type=F32)
    o_ref[...] = acc.astype(o_ref.dtype)


def merge_branches(ya, yc, ys, zrest, col_g, w_a, w_c, w_s, layer, d):
    m = ya.shape[0]
    bm = _pick(m, 512)
    bn = _pick(d, 512)
    og = col_g // bn
    per = d // bn
    yspec = lambda y: pl.BlockSpec((bm, y.shape[1]), lambda j, i: (i, 0))
    gspec = lambda k: pl.BlockSpec((bm, bn), lambda j, i: (i, og + k * per + j))
    wspec = lambda w: pl.BlockSpec((None, w.shape[1], bn), lambda j, i: (layer, 0, j))
    return pl.pallas_call(
        _merge_kernel,
        grid=(per, m // bm),
        in_specs=[yspec(ya), yspec(yc), yspec(ys), gspec(0), gspec(1), gspec(2), wspec(w_a), wspec(w_c), wspec(w_s)],
        out_specs=pl.BlockSpec((bm, bn), lambda j, i: (i, j)),
        out_shape=jax.ShapeDtypeStruct((m, d), BF16),
        scratch_shapes=[pltpu.VMEM((w_a.shape[1], bn), BF16), pltpu.VMEM((w_c.shape[1], bn), BF16),
                        pltpu.VMEM((w_s.shape[1], bn), BF16)],
        compiler_params=_cparams("arbitrary", "arbitrary"),
        name="merge_branches",
    )(ya, yc, ys, zrest, zrest, zrest, w_a, w_c, w_s)


def _gather_lane_tiles(ref, lead=()):
    return jnp.concatenate([ref[lead + (slice(None), c, slice(None))] for c in range(ref.shape[-2])], axis=1)


def _scatter_lane_tiles(ref, val, lead=()):
    for c in range(ref.shape[-2]):
        ref[lead + (slice(None), c, slice(None))] = val[:, c * LANES:(c + 1) * LANES]


def _layer_norm(v, g, b):
    mu = jnp.mean(v, axis=-1, keepdims=True)
    var = jnp.mean(jnp.square(v - mu), axis=-1, keepdims=True)
    return (v - mu) * lax.rsqrt(var + LN_EPS) * g + b


def _ln_route_kernel(x_ref, y_ref, gate_ref, g_ref, b_ref, sc_ref, sh_ref, wr_ref, br_ref,
                     x1_ref, tok_ref, eidx_ref, wts_ref, *, alpha, n_groups, per_group):
    x1 = _layer_norm(alpha * x_ref[0] + (1.0 + gate_ref[0]) * y_ref[0], g_ref[...], b_ref[...])
    x1_ref[0] = x1
    tok = x1 * (1.0 + sc_ref[0]) + sh_ref[0]
    _scatter_lane_tiles(tok_ref, tok, (0,))
    logit = jnp.dot(tok.astype(BF16), wr_ref[...].astype(BF16), preferred_element_type=F32) + br_ref[...]
    lane_i = lax.broadcasted_iota(I32, logit.shape, 1)
    lane = lane_i.astype(F32)
    big = float(LANES)
    neg = -jnp.inf
    gl = jnp.where(lane < n_groups, logit, neg)
    gmax = jnp.max(gl, axis=1, keepdims=True)
    gidx = jnp.min(jnp.where(gl == gmax, lane, big), axis=1, keepdims=True)
    gprob = 1.0 / jnp.sum(jnp.exp(gl - gmax), axis=1, keepdims=True)
    lo = n_groups + gidx * per_group
    el = jnp.where((lane >= lo) & (lane < lo + per_group), logit, neg)
    t1 = jnp.max(el, axis=1, keepdims=True)
    i1 = jnp.min(jnp.where(el == t1, lane, big), axis=1, keepdims=True)
    el2 = jnp.where(lane == i1, neg, el)
    t2 = jnp.max(el2, axis=1, keepdims=True)
    i2 = jnp.min(jnp.where(el2 == t2, lane, big), axis=1, keepdims=True)
    e2 = jnp.exp(t2 - t1)
    w1 = gprob / (1.0 + e2)
    w2 = gprob * e2 / (1.0 + e2)
    eidx_ref[0] = jnp.where(lane_i == 0, i1 - n_groups, jnp.where(lane_i == 1, i2 - n_groups, 0.0)).astype(I32)
    wts_ref[0] = jnp.where(lane_i == 0, w1, jnp.where(lane_i == 1, w2, 0.0))


def ln_route(x3, y3, gate, ln_g, ln_b, sc, sh, w_router, b_router, alpha, n_groups, per_group):
    nb, t, d = x3.shape
    bt = _pick(t, 256)
    blk = pl.BlockSpec((1, bt, d), lambda b, i: (b, i, 0))
    row = pl.BlockSpec((1, d), lambda b, i: (0, 0))
    sel = pl.BlockSpec((1, bt, LANES), lambda b, i: (b, i, 0))
    tiles = pl.BlockSpec((1, bt, d // LANES, LANES), lambda b, i: (b, i, 0, 0))
    kern = functools.partial(_ln_route_kernel, alpha=alpha, n_groups=n_groups, per_group=per_group)
    return pl.pallas_call(
        kern,
        grid=(nb, t // bt),
        in_specs=[blk, blk, _mod_spec(gate, bt), row, row, _mod_spec(sc, bt), _mod_spec(sh, bt),
                  pl.BlockSpec((d, LANES), lambda b, i: (0, 0)), pl.BlockSpec((1, LANES), lambda b, i: (0, 0))],
        out_specs=[blk, tiles, sel, sel],
        out_shape=[jax.ShapeDtypeStruct((nb, t, d), F32), jax.ShapeDtypeStruct((nb, t, d // LANES, LANES), F32),
                   jax.ShapeDtypeStruct((nb, t, LANES), I32), jax.ShapeDtypeStruct((nb, t, LANES), F32)],
        compiler_params=_cparams("arbitrary", "arbitrary"),
        name="ln_route",
    )(x3, y3, gate, ln_g, ln_b, sc, sh, w_router, b_router)


def _ln_combine_kernel(x_ref, y0_ref, y1_ref, wts_ref, gate_ref, g_ref, b_ref, o_ref, *, alpha):
    w = wts_ref[0]
    ffn = w[:, 0:1] * _gather_lane_tiles(y0_ref, (0,)) + w[:, 1:2] * _gather_lane_tiles(y1_ref, (0,))
    o_ref[0] = _layer_norm(alpha * x_ref[0] + (1.0 + gate_ref[0]) * ffn, g_ref[...], b_ref[...])


def ln_combine(x3, y_assign, wts, gate, ln_g, ln_b, alpha):
    nb, t, d = x3.shape
    bt = _pick(t, 256)
    blk = pl.BlockSpec((1, bt, d), lambda b, i: (b, i, 0))
    row = pl.BlockSpec((1, d), lambda b, i: (0, 0))
    ysp = lambda k: pl.BlockSpec((None, 1, bt, d // LANES, LANES), lambda b, i: (k, b, i, 0, 0))
    kern = functools.partial(_ln_combine_kernel, alpha=alpha)
    return pl.pallas_call(
        kern,
        grid=(nb, t // bt),
        in_specs=[blk, ysp(0), ysp(1), pl.BlockSpec((1, bt, LANES), lambda b, i: (b, i, 0)),
                  _mod_spec(gate, bt), row, row],
        out_specs=blk,
        out_shape=jax.ShapeDtypeStruct((nb, t, d), F32),
        compiler_params=_cparams("arbitrary", "arbitrary"),
        name="ln_combine",
    )(x3, y_assign, y_assign, wts, gate, ln_g, ln_b)


def _rank_kernel(e_ref, rank_ref, cnt_ref, car_ref):
    first = (pl.program_id(0) == 0) & (pl.program_id(1) == 0)

    @pl.when(first)
    def _():
        car_ref[...] = jnp.zeros_like(car_ref)

    e = e_ref[...]
    rb = e.shape[1]
    ex = lax.broadcasted_iota(I32, (LANES, rb), 0)
    onehot = jnp.where(ex == e, 1.0, 0.0).astype(BF16)
    r = lax.broadcasted_iota(I32, (rb, rb), 0)
    c = lax.broadcasted_iota(I32, (rb, rb), 1)
    upto = jnp.where(r <= c, 1.0, 0.0).astype(BF16)
    cum = jnp.dot(onehot, upto, preferred_element_type=F32)
    oh = onehot.astype(F32)
    rank = jnp.sum(oh * (cum - 1.0 + car_ref[...]), axis=0, keepdims=True)
    rank_ref[...] = rank.astype(I32)
    car_ref[...] = car_ref[...] + jnp.sum(oh, axis=1, keepdims=True)
    cnt_ref[...] = jnp.broadcast_to(car_ref[...], cnt_ref.shape).astype(I32)


def expert_ranks(e_rows):
    two, nblk, _, rb = e_rows.shape
    return pl.pallas_call(
        _rank_kernel,
        grid=(two, nblk),
        in_specs=[pl.BlockSpec((None, None, 1, rb), lambda k, i: (k, i, 0, 0))],
        out_specs=[pl.BlockSpec((None, None, 1, rb), lambda k, i: (k, i, 0, 0)),
                   pl.BlockSpec((LANES, LANES), lambda k, i: (0, 0))],
        out_shape=[jax.ShapeDtypeStruct(e_rows.shape, I32), jax.ShapeDtypeStruct((LANES, LANES), I32)],
        scratch_shapes=[pltpu.VMEM((LANES, 1), F32)],
        compiler_params=_cparams("arbitrary", "arbitrary"),
        name="expert_ranks",
    )(e_rows)


def _expert_kernel(dst_ref, be_ref, nu_ref, tok_hbm, wg_ref, wu_ref, wd_ref, spill_in, y_hbm, spill_hbm,
                   xbuf, ybuf, gsem, ssem, *, rows, nblk, n_tok):
    del spill_in
    del be_ref
    i = pl.program_id(0)
    slot = lax.rem(i, 2)
    n_used = nu_ref[0]
    n_asg = 2 * n_tok

    def fetch(blk, sl):
        @pl.when(blk < n_used)
        def _():
            def body(u, c):
                a = dst_ref[blk * rows + u]
                pltpu.make_async_copy(tok_hbm.at[pl.ds(lax.rem(a, n_tok), 1)], xbuf.at[sl, pl.ds(u, 1)],
                                      gsem.at[sl]).start()
                return c

            lax.fori_loop(0, rows, body, 0, unroll=8)

    def store(blk, sl):
        @pl.when(blk < n_used)
        def _():
            def body(u, c):
                a = dst_ref[blk * rows + u]
                src = ybuf.at[sl, pl.ds(u, 1)]

                @pl.when(a < n_asg)
                def _():
                    pltpu.make_async_copy(src, y_hbm.at[pl.ds(a, 1)], ssem.at[sl]).start()

                @pl.when(a >= n_asg)
                def _():
                    pltpu.make_async_copy(src, spill_hbm.at[pl.ds(a - n_asg, 1)], ssem.at[sl]).start()

                return c

            lax.fori_loop(0, rows, body, 0, unroll=8)

    def wait_block(blk, buf, sem, sl):
        @pl.when(blk < n_used)
        def _():
            pltpu.make_async_copy(buf.at[sl], buf.at[sl], sem.at[sl]).wait()

    @pl.when(i == 0)
    def _():
        fetch(0, 0)

    @pl.when(i + 1 < nblk)
    def _():
        fetch(i + 1, 1 - slot)

    wait_block(i, xbuf, gsem, slot)

    @pl.when(i >= 2)
    def _():
        wait_block(i - 2, ybuf, ssem, slot)

    @pl.when(i < n_used)
    def _():
        xb = _gather_lane_tiles(xbuf.at[slot]).astype(BF16)
        g = jnp.dot(xb, wg_ref[...], preferred_element_type=F32)
        u = jnp.dot(xb, wu_ref[...], preferred_element_type=F32)
        h = (g * jax.nn.sigmoid(g) * u).astype(BF16)
        _scatter_lane_tiles(ybuf.at[slot], jnp.dot(h, wd_ref[...], preferred_element_type=F32))

    store(i, slot)

    @pl.when(i == nblk - 1)
    def _():
        if nblk >= 2:
            wait_block(i - 1, ybuf, ssem, 1 - slot)
        wait_block(i, ybuf, ssem, slot)


def expert_blocks(tok, dst_sorted, blk_e, n_used, wg, wu, wd, layer, rows):
    n, nc, _ = tok.shape
    d = nc * LANES
    hid = wg.shape[-1]
    nblk = dst_sorted.shape[0] // rows
    live = lambda i, nu: jnp.minimum(i, nu[0] - 1)
    hbm = pl.BlockSpec(memory_space=pl.ANY)
    kern = functools.partial(_expert_kernel, rows=rows, nblk=nblk, n_tok=n)
    spill = jnp.zeros((2 * rows, nc, LANES), F32)
    y, _ = pl.pallas_call(
        kern,
        grid_spec=pltpu.PrefetchScalarGridSpec(
            num_scalar_prefetch=3,
            grid=(nblk,),
            in_specs=[hbm,
                      pl.BlockSpec((None, None, d, hid), lambda i, ds, be, nu: (layer, be[live(i, nu)], 0, 0)),
                      pl.BlockSpec((None, None, d, hid), lambda i, ds, be, nu: (layer, be[live(i, nu)], 0, 0)),
                      pl.BlockSpec((None, None, hid, d), lambda i, ds, be, nu: (layer, be[live(i, nu)], 0, 0)),
                      hbm],
            out_specs=[hbm, hbm],
            scratch_shapes=[pltpu.VMEM((2, rows, nc, LANES), F32), pltpu.VMEM((2, rows, nc, LANES), F32),
                            pltpu.SemaphoreType.DMA((2,)), pltpu.SemaphoreType.DMA((2,))],
        ),
        out_shape=[jax.ShapeDtypeStruct((2 * n, nc, LANES), F32), jax.ShapeDtypeStruct(spill.shape, F32)],
        input_output_aliases={7: 1},
        compiler_params=_cparams("arbitrary"),
        name="expert_blocks",
    )(dst_sorted, blk_e, n_used, tok, wg, wu, wd, spill)
    return y


def hierarchical_moe(tok3, eidx, wg, wu, wd, layer, rows):
    nb, t, nc, _ = tok3.shape
    n = nb * t
    n_exp = wg.shape[1]
    e2 = eidx.reshape(n, LANES)[:, :2].T
    if n >= LANES:
        rb = _pick(n, 512)
        rank, cnt = expert_ranks(e2.reshape(2, n // rb, 1, rb))
        rank = rank.reshape(-1)
        counts = cnt[:n_exp, 0]
    else:
        ef = e2.reshape(-1)
        ar = jnp.arange(2 * n)
        rank = jnp.sum((ef[:, None] == ef[None, :]) & (ar[None, :] < ar[:, None]), axis=1).astype(I32)
        counts = jnp.sum(ef[:, None] == jnp.arange(n_exp)[None, :], axis=0).astype(I32)
    ef = e2.reshape(-1)
    nblk = min(-(-(2 * n + n_exp * (rows - 1)) // rows), 2 * n)
    padded = (counts + rows - 1) // rows * rows
    pad_end = jnp.cumsum(padded)
    dest = ((pad_end - padded)[ef] + rank).astype(I32)
    r = jnp.arange(nblk * rows, dtype=I32)
    spill_row = 2 * n + (r // rows) % 2 * rows + r % rows
    dst_sorted = spill_row.at[dest].set(jnp.arange(2 * n, dtype=I32))
    blk_e = jnp.minimum(jnp.sum(pad_end[None, :] <= (jnp.arange(nblk) * rows)[:, None], axis=1), n_exp - 1).astype(I32)
    n_used = (pad_end[-1:] // rows).astype(I32)
    y_assign = expert_blocks(tok3.reshape(n, nc, LANES), dst_sorted, blk_e, n_used, wg, wu, wd, layer, rows)
    return y_assign.reshape(2, nb, t, nc, LANES)


def _dims(p):
    d = p['w_o'].shape[-1]
    aw = p['w_br_attn'].shape[1]
    cw = p['w_br_conv'].shape[1]
    sw = p['w_br_ssm'].shape[1]
    return d, aw, cw, sw


def _input_projection(xin, xin_s, p, layer, nb, t, heads):
    d, aw, cw, sw = _dims(p)
    wt = p['w_in_t']
    base = layer * (wt.shape[0] // p['w_o'].shape[0])
    zqkv, zqkv_s = matmul_nt(xin, wt, base, 3 * aw, a2=xin_s)
    zrest, zrest_s = matmul_nt(xin, wt, base + 3 * aw + heads, 3 * cw + sw + 3 * d, a2=xin_s)
    b_pad = jnp.zeros((1, LANES), F32).at[0, :heads].set(p['b_forget'][layer])
    logf, fcum = forget_gate(xin, wt, base + 3 * aw, b_pad, nb, t, True)
    logf_s, _ = forget_gate(xin_s, wt, base + 3 * aw, b_pad, 1, xin_s.shape[0], False)
    return (zqkv, zrest, logf, fcum), (zqkv_s, zrest_s, logf_s)


def _finish_mixer(x3, ya, yc, ys, zrest, p, layer, mods, alpha, n_groups, per_group, w_router, b_router):
    d, aw, cw, sw = _dims(p)
    nb, t, _ = x3.shape
    merged = merge_branches(ya, yc, ys, zrest, 3 * cw + sw, p['w_br_attn'], p['w_br_conv'], p['w_br_ssm'], layer, d)
    mix = matmul(merged, p['w_o'], layer)
    sh1, sc1, g1, sh2, sc2, g2 = mods
    return ln_route(x3, mix.reshape(nb, t, d), g1, p['ln1_g'][layer][None], p['ln1_b'][layer][None], sc2, sh2,
                    w_router, b_router, alpha, n_groups, per_group)


def _ssm_tiles(state, nt):
    nb = state.shape[0]
    return state.reshape(nb, nt, -1).transpose(1, 0, 2)


def _ssm_untile(h, nb, g, p):
    return h.transpose(1, 0, 2).reshape(nb, g, p)


def kernel(x_prompt, x_sample, c_prompt, c_sample, cache_k, cache_v, cache_logf, page_table, state_conv, state_ssm_re, state_ssm_im, w_ada, b_ada, ln1_g, ln1_b, ln2_g, ln2_b, w_in, b_forget, conv_w, ssm_lambda_re, ssm_lambda_im, ssm_log_dt, ssm_b_re, ssm_b_im, ssm_c_re, ssm_c_im, ssm_d, ssm_w_glu, w_br_attn, w_br_conv, w_br_ssm, w_o, router_w_group, router_b_group, router_w_expert, router_b_expert, moe_w_gate, moe_w_up, moe_w_down):
    p = dict(b_forget=b_forget, w_br_attn=w_br_attn, w_br_conv=w_br_conv, w_br_ssm=w_br_ssm, w_o=w_o,
             ln1_g=ln1_g, ln1_b=ln1_b, ln2_g=ln2_g, ln2_b=ln2_b)
    depth = w_in.shape[0]
    nbp, t, d = x_prompt.shape
    nbs, ts, _ = x_sample.shape
    assert ts == 1
    heads, dh = cache_k.shape[3], cache_k.shape[4]
    aw = heads * dh
    cw = conv_w.shape[-1]
    sw = ssm_d.shape[-1]
    g, pst = ssm_lambda_re.shape[1], ssm_lambda_re.shape[2]
    gc = ssm_b_re.shape[-1]
    nt = g * gc // LANES
    ns = (LANES // gc) * pst
    n_groups = router_w_group.shape[-1]
    n_exp = router_w_expert.shape[-1]
    per_group = n_exp // n_groups
    alpha = (2 * depth) ** 0.25
    chunk = SSM_CHUNK
    col_conv = 0
    col_u = 3 * cw

    r = nbp + nbs
    rpad = -(-r // 8) * 8
    c_all = jnp.zeros((rpad, d), F32).at[:nbp].set(c_prompt).at[nbp:r].set(c_sample)
    mod_all = ada_modulation(c_all, w_ada, b_ada)

    assert w_in.shape[2] % 8 == 0 and (3 * aw + heads) % 8 == 0
    p['w_in_t'] = jnp.swapaxes(w_in, 1, 2).reshape(depth * w_in.shape[2], d)

    wg_bf = moe_w_gate.astype(BF16)
    wu_bf = moe_w_up.astype(BF16)
    wd_bf = moe_w_down.astype(BF16)

    xp = x_prompt
    xs = x_sample.reshape(1, nbs, d)
    outs = {k: [] for k in ('kp', 'vp', 'fp', 'ks', 'vs', 'fs', 'cp', 'cs', 'srp', 'sip', 'srs', 'sis')}
    zero_conv = jnp.zeros((nbp, 2, cw), F32)
    zero_h = jnp.zeros((nt, nbp, 2 * ns), F32)
    for l in range(depth):
        mp = [mod_all[l, :nbp, i * d:(i + 1) * d].reshape(nbp, 1, d) for i in range(6)]
        ms = [mod_all[l, nbp:r, i * d:(i + 1) * d].reshape(1, nbs, d) for i in range(6)]
        w_router = jnp.zeros((d, LANES), F32).at[:, :n_groups].set(router_w_group[l]) \
            .at[:, n_groups:n_groups + n_exp].set(router_w_expert[l])
        b_router = jnp.zeros((1, LANES), F32).at[0, :n_groups].set(router_b_group[l]) \
            .at[0, n_groups:n_groups + n_exp].set(router_b_expert[l])
        wts = ssm_weights(ssm_lambda_re[l], ssm_lambda_im[l], ssm_log_dt[l], ssm_b_re[l], ssm_b_im[l],
                          ssm_c_re[l], ssm_c_im[l], chunk)
        d_t = ssm_d[l].reshape(nt, 1, LANES)
        d_cat = jnp.tile(d_t, (1, 1, chunk))
        eye = jnp.eye(LANES // gc, dtype=F32)
        cdiag = lambda c: jnp.einsum('tgcp,gh->thcgp', c.reshape(nt, LANES // gc, gc, pst), eye).reshape(nt, LANES, ns)
        cmat = jnp.concatenate([cdiag(ssm_c_re[l]), -cdiag(ssm_c_im[l])], axis=2).astype(BF16)

        n = nbp * t
        xin = modulate(xp, mp[1], mp[0]).reshape(n, d)
        xin_s = modulate(xs, ms[1], ms[0]).reshape(nbs, d)
        (zqkv, zrest, logf, fcum), (zqkv_s, zrest_s, logf_s) = _input_projection(xin, xin_s, p, l, nbp, t, heads)
        fcum_t = fcum[:, :heads].reshape(nbp, t, heads).transpose(0, 2, 1)
        ya = flash_attention(zqkv, fcum, fcum_t, nbp, t, heads, dh)
        yc, conv_p = short_conv_prompt(zrest, col_conv, conv_w[l], zero_conv, nbp, t, cw)
        zs, hfin = ssm_prompt(zrest.reshape(nbp, t, -1), col_u, wts, zero_h, d_cat, chunk)
        ys = glu(zs.reshape(n, sw), ssm_w_glu, l)
        x1, tok, eidx, wsel = _finish_mixer(xp, ya, yc, ys, zrest, p, l, mp, alpha, n_groups, per_group,
                                            w_router, b_router)
        y_assign = hierarchical_moe(tok, eidx, wg_bf, wu_bf, wd_bf, l, MOE_ROWS)
        xp = ln_combine(x1, y_assign, wsel, mp[5], ln2_g[l][None], ln2_b[l][None], alpha)
        outs['kp'].append(zqkv[:, aw:2 * aw].reshape(nbp, t, heads, dh))
        outs['vp'].append(zqkv[:, 2 * aw:].reshape(nbp, t, heads, dh))
        outs['fp'].append(logf[:, :heads].reshape(nbp, t, heads))
        outs['cp'].append(conv_p)
        outs['srp'].append(_ssm_untile(hfin[:, :, :ns], nbp, g, pst))
        outs['sip'].append(_ssm_untile(hfin[:, :, ns:], nbp, g, pst))

        q_s = zqkv_s[:, :aw].reshape(nbs, heads, dh)
        k_s = zqkv_s[:, aw:2 * aw].reshape(nbs, heads, dh)
        v_s = zqkv_s[:, 2 * aw:].reshape(nbs, heads, dh)
        ya_s = decode_attention(q_s, k_s, v_s, logf_s[:, :heads].reshape(nbs, 1, heads), cache_k, cache_v, cache_logf,
                                page_table, l)
        ya_s = ya_s.reshape(nbs, aw).astype(BF16)
        yc_s, v_row = short_conv_step(zrest_s, col_conv, conv_w[l], state_conv[l, :, 0], state_conv[l, :, 1], cw)
        h0 = jnp.concatenate([_ssm_tiles(state_ssm_re[l], nt), _ssm_tiles(state_ssm_im[l], nt)], axis=2)
        zs_s, h_s = ssm_step(zrest_s, col_u, wts, cmat, h0, d_t, chunk)
        ys_s = glu(zs_s, ssm_w_glu, l)
        x1_s, tok_s, eidx_s, wsel_s = _finish_mixer(xs, ya_s, yc_s, ys_s, zrest_s, p, l, ms, alpha, n_groups,
                                                    per_group, w_router, b_router)
        y_assign_s = hierarchical_moe(tok_s, eidx_s, wg_bf, wu_bf, wd_bf, l, 8)
        xs = ln_combine(x1_s, y_assign_s, wsel_s, ms[5], ln2_g[l][None], ln2_b[l][None], alpha)
        outs['ks'].append(k_s.reshape(nbs, 1, heads, dh))
        outs['vs'].append(v_s.reshape(nbs, 1, heads, dh))
        outs['fs'].append(logf_s[:, :heads].reshape(nbs, 1, heads))
        outs['cs'].append(jnp.stack([state_conv[l, :, 1], v_row], axis=1))
        outs['srs'].append(_ssm_untile(h_s[:, :, :ns], nbs, g, pst))
        outs['sis'].append(_ssm_untile(h_s[:, :, ns:], nbs, g, pst))

    st = lambda k: jnp.stack(outs[k])
    return (xp, xs.reshape(nbs, 1, d), st('kp'), st('vp'), st('fp'), st('ks'), st('vs'), st('fs'),
            st('cp'), st('cs'), st('srp'), st('sip'), st('srs'), st('sis'))
```

```python
import functools
import math

import jax
import jax.numpy as jnp
from jax import lax
from jax.experimental import pallas as pl
from jax.experimental.pallas import tpu as pltpu

F32 = jnp.float32
BF16 = jnp.bfloat16
I32 = jnp.int32

LANES = 128
VMEM_LIMIT = 56 * 1024 * 1024
LN_EPS = 1e-5
SSM_CHUNK = 8
MOE_ROWS = 128
NT_DIMS = (((1,), (1,)), ((), ()))


def _cparams(*sem):
    return pltpu.CompilerParams(dimension_semantics=sem, vmem_limit_bytes=VMEM_LIMIT)


def _pick(n, pref):
    if n <= pref:
        return n
    b = pref
    while n % b:
        b //= 2
    return b


def _ada_kernel(c_ref, w_ref, b_ref, o_ref):
    c = c_ref[...]
    s = (c * jax.nn.sigmoid(c)).astype(BF16)
    o_ref[...] = jnp.dot(s, w_ref[...].astype(BF16), preferred_element_type=F32) + b_ref[...]


def ada_modulation(c_all, w_ada, b_ada):
    depth, d, n6 = w_ada.shape
    r = c_all.shape[0]
    bn = _pick(n6, 512)
    return pl.pallas_call(
        _ada_kernel,
        grid=(depth, n6 // bn),
        in_specs=[
            pl.BlockSpec((r, d), lambda l, j: (0, 0)),
            pl.BlockSpec((None, d, bn), lambda l, j: (l, 0, j)),
            pl.BlockSpec((None, 1, bn), lambda l, j: (l, 0, j)),
        ],
        out_specs=pl.BlockSpec((None, r, bn), lambda l, j: (l, 0, j)),
        out_shape=jax.ShapeDtypeStruct((depth, r, n6), F32),
        compiler_params=_cparams("arbitrary", "arbitrary"),
        name="ada_modulation",
    )(c_all, w_ada, b_ada.reshape(depth, 1, n6))


def _mod_spec(mod, bt):
    if mod.shape[1] == 1:
        return pl.BlockSpec((1, 1, mod.shape[2]), lambda b, t: (b, 0, 0))
    return pl.BlockSpec((1, bt, mod.shape[2]), lambda b, t: (b, t, 0))


def _modulate_kernel(x_ref, sc_ref, sh_ref, o_ref):
    o_ref[...] = (x_ref[...] * (1.0 + sc_ref[...]) + sh_ref[...]).astype(o_ref.dtype)


def modulate(x3, sc, sh):
    nb, t, d = x3.shape
    bt = _pick(t, 512)
    return pl.pallas_call(
        _modulate_kernel,
        grid=(nb, t // bt),
        in_specs=[pl.BlockSpec((1, bt, d), lambda b, i: (b, i, 0)), _mod_spec(sc, bt), _mod_spec(sh, bt)],
        out_specs=pl.BlockSpec((1, bt, d), lambda b, i: (b, i, 0)),
        out_shape=jax.ShapeDtypeStruct((nb, t, d), BF16),
        compiler_params=_cparams("arbitrary", "arbitrary"),
        name="modulate",
    )(x3, sc, sh)


def _mm_kernel(a_ref, w_ref, o_ref, wbf_ref):
    @pl.when(pl.program_id(1) == 0)
    def _():
        wbf_ref[...] = w_ref[...].astype(BF16)

    o_ref[...] = jnp.dot(a_ref[...].astype(BF16), wbf_ref[...], preferred_element_type=F32).astype(o_ref.dtype)


def matmul(a, w3, layer, *, out_dtype=F32, bm=1024, bn=512):
    m, k = a.shape
    ncols = w3.shape[2]
    bm = _pick(m, bm)
    bn = _pick(ncols, bn)
    assert ncols % bn == 0 and m % bm == 0
    return pl.pallas_call(
        _mm_kernel,
        grid=(ncols // bn, m // bm),
        in_specs=[
            pl.BlockSpec((bm, k), lambda j, i: (i, 0)),
            pl.BlockSpec((None, k, bn), lambda j, i: (layer, 0, j)),
        ],
        out_specs=pl.BlockSpec((bm, bn), lambda j, i: (i, j)),
        out_shape=jax.ShapeDtypeStruct((m, ncols), out_dtype),
        scratch_shapes=[pltpu.VMEM((k, bn), BF16)],
        compiler_params=_cparams("arbitrary", "arbitrary"),
        name="matmul",
    )(a, w3)


def _mm_nt_kernel(a_ref, w_ref, o_ref, wbf_ref):
    @pl.when(pl.program_id(1) == 0)
    def _():
        wbf_ref[...] = w_ref[...].astype(BF16)

    o_ref[...] = lax.dot_general(a_ref[...].astype(BF16), wbf_ref[...], NT_DIMS,
                                 preferred_element_type=F32).astype(o_ref.dtype)


def _weight_rows(row0, bn, k):
    return pl.BlockSpec((pl.Element(bn), pl.Element(k)), lambda j, i: (pl.multiple_of(row0 + j * bn, 8), 0))


def _mm_nt2_kernel(a_ref, a2_ref, w_ref, o_ref, o2_ref, wbf_ref):
    @pl.when(pl.program_id(1) == 0)
    def _():
        wbf_ref[...] = w_ref[...].astype(BF16)
        o2_ref[...] = lax.dot_general(a2_ref[...].astype(BF16), wbf_ref[...], NT_DIMS,
                                      preferred_element_type=F32).astype(o2_ref.dtype)

    o_ref[...] = lax.dot_general(a_ref[...].astype(BF16), wbf_ref[...], NT_DIMS,
                                 preferred_element_type=F32).astype(o_ref.dtype)


def matmul_nt(a, wt, row0, ncols, *, a2=None, out_dtype=F32, bm=1024, bn=512):
    m, k = a.shape
    bm = _pick(m, bm)
    bn = _pick(ncols, bn)
    assert ncols % bn == 0 and m % bm == 0 and row0 % 8 == 0 and row0 + ncols <= wt.shape[0]
    grid = (ncols // bn, m // bm)
    a_spec = pl.BlockSpec((bm, k), lambda j, i: (i, 0))
    o_spec = pl.BlockSpec((bm, bn), lambda j, i: (i, j))
    scratch = [pltpu.VMEM((bn, k), BF16)]
    if a2 is None:
        return pl.pallas_call(
            _mm_nt_kernel,
            grid=grid,
            in_specs=[a_spec, _weight_rows(row0, bn, k)],
            out_specs=o_spec,
            out_shape=jax.ShapeDtypeStruct((m, ncols), out_dtype),
            scratch_shapes=scratch,
            compiler_params=_cparams("arbitrary", "arbitrary"),
            name="matmul_nt",
        )(a, wt)
    m2 = a2.shape[0]
    return pl.pallas_call(
        _mm_nt2_kernel,
        grid=grid,
        in_specs=[a_spec, pl.BlockSpec((m2, k), lambda j, i: (0, 0)), _weight_rows(row0, bn, k)],
        out_specs=[o_spec, pl.BlockSpec((m2, bn), lambda j, i: (0, j))],
        out_shape=[jax.ShapeDtypeStruct((m, ncols), out_dtype), jax.ShapeDtypeStruct((m2, ncols), out_dtype)],
        scratch_shapes=scratch,
        compiler_params=_cparams("arbitrary", "arbitrary"),
        name="matmul_nt",
    )(a, a2, wt)


def _forget_kernel(a_ref, w_ref, b_ref, lf_ref, fc_ref, carry_ref, *, cumsum):
    t = pl.program_id(1)
    z = lax.dot_general(a_ref[...].astype(BF16), w_ref[...].astype(BF16), NT_DIMS,
                        preferred_element_type=F32) + b_ref[...]
    lf = jnp.minimum(z, 0.0) - jnp.log1p(jnp.exp(-jnp.abs(z)))
    lf_ref[...] = lf
    if not cumsum:
        fc_ref[...] = lf
        return

    @pl.when(t == 0)
    def _():
        carry_ref[...] = jnp.zeros_like(carry_ref)

    bt = lf.shape[0]
    row = lax.broadcasted_iota(I32, lf.shape, 0)
    acc = lf
    s = 1
    while s < bt:
        acc = acc + jnp.where(row >= s, pltpu.roll(acc, s, axis=0), 0.0)
        s *= 2
    acc = acc + carry_ref[...]
    fc_ref[...] = acc
    carry_ref[...] = acc[bt - 1:bt, :]


def forget_gate(xin, wt, row0, b_pad, nb, t, cumsum):
    n, d = xin.shape
    bt = _pick(t, 512)
    nt = t // bt
    assert row0 % 8 == 0 and row0 + LANES <= wt.shape[0]
    kern = functools.partial(_forget_kernel, cumsum=cumsum)
    return pl.pallas_call(
        kern,
        grid=(nb, nt),
        in_specs=[
            pl.BlockSpec((bt, d), lambda b, i: (b * nt + i, 0)),
            pl.BlockSpec((pl.Element(LANES), pl.Element(d)), lambda b, i: (row0, 0)),
            pl.BlockSpec((1, LANES), lambda b, i: (0, 0)),
        ],
        out_specs=[pl.BlockSpec((bt, LANES), lambda b, i: (b * nt + i, 0))] * 2,
        out_shape=[jax.ShapeDtypeStruct((n, LANES), F32)] * 2,
        scratch_shapes=[pltpu.VMEM((1, LANES), F32)],
        compiler_params=_cparams("arbitrary", "arbitrary"),
        name="forget_gate",
    )(xin, wt, b_pad)


def _flash_kernel(qi_ref, ki_ref, q_ref, k_ref, v_ref, fq_ref, fk_ref, o_ref, m_ref, l_ref, acc_ref, fqc_ref,
                  *, hp, dh):
    hb = pl.program_id(1) * hp
    step = pl.program_id(2)
    qi = qi_ref[step]
    ki = ki_ref[step]

    @pl.when(ki == 0)
    def _():
        m_ref[...] = jnp.full_like(m_ref, -jnp.inf)
        l_ref[...] = jnp.zeros_like(l_ref)
        acc_ref[...] = jnp.zeros_like(acc_ref)
        lane = lax.broadcasted_iota(I32, fq_ref.shape, 1)
        for hh in range(hp):
            fqc_ref[hh] = jnp.sum(jnp.where(lane == hb + hh, fq_ref[...], 0.0), axis=1, keepdims=True)

    def block(masked):
        for hh in range(hp):
            cols = slice(hh * dh, (hh + 1) * dh)
            s = lax.dot_general(q_ref[:, cols].astype(BF16), k_ref[:, cols].astype(BF16), NT_DIMS,
                                preferred_element_type=F32) * (dh ** -0.5)
            s = s + fqc_ref[hh] - fk_ref[pl.ds(hb + hh, 1), :]
            if masked:
                row = lax.broadcasted_iota(I32, s.shape, 0)
                col = lax.broadcasted_iota(I32, s.shape, 1)
                s = jnp.where(col <= row, s, -jnp.inf)
            m_prev = m_ref[hh]
            m_new = jnp.maximum(m_prev, jnp.max(s, axis=1, keepdims=True))
            alpha = jnp.exp(m_prev - m_new)
            p = jnp.exp(s - m_new)
            l_ref[hh] = alpha * l_ref[hh] + jnp.sum(p, axis=1, keepdims=True)
            acc_ref[hh] = alpha * acc_ref[hh] + jnp.dot(p.astype(BF16), v_ref[:, cols].astype(BF16),
                                                        preferred_element_type=F32)
            m_ref[hh] = m_new

    @pl.when(ki < qi)
    def _():
        block(False)

    @pl.when(ki == qi)
    def _():
        block(True)
        for hh in range(hp):
            o_ref[:, hh * dh:(hh + 1) * dh] = (acc_ref[hh] / l_ref[hh]).astype(o_ref.dtype)


def flash_attention(zqkv, fcum, fcum_t, nb, t, heads, dh):
    n = nb * t
    bq = _pick(t, 512)
    nq = t // bq
    hp = 2 if heads % 2 == 0 else 1
    ng = heads // hp
    pairs = [(i, j) for i in range(nq) for j in range(i + 1)]
    qi_tab = jnp.asarray([p[0] for p in pairs], I32)
    ki_tab = jnp.asarray([p[1] for p in pairs], I32)
    kern = functools.partial(_flash_kernel, hp=hp, dh=dh)
    w = hp * dh
    return pl.pallas_call(
        kern,
        grid_spec=pltpu.PrefetchScalarGridSpec(
            num_scalar_prefetch=2,
            grid=(nb, ng, len(pairs)),
            in_specs=[
                pl.BlockSpec((bq, w), lambda b, g, s, qt, kt: (b * nq + qt[s], g)),
                pl.BlockSpec((bq, w), lambda b, g, s, qt, kt: (b * nq + kt[s], ng + g)),
                pl.BlockSpec((bq, w), lambda b, g, s, qt, kt: (b * nq + kt[s], 2 * ng + g)),
                pl.BlockSpec((bq, LANES), lambda b, g, s, qt, kt: (b * nq + qt[s], 0)),
                pl.BlockSpec((None, heads, bq), lambda b, g, s, qt, kt: (b, 0, kt[s])),
            ],
            out_specs=pl.BlockSpec((bq, w), lambda b, g, s, qt, kt: (b * nq + qt[s], g)),
            scratch_shapes=[pltpu.VMEM((hp, bq, 1), F32), pltpu.VMEM((hp, bq, 1), F32),
                            pltpu.VMEM((hp, bq, dh), F32), pltpu.VMEM((hp, bq, 1), F32)],
        ),
        out_shape=jax.ShapeDtypeStruct((n, heads * dh), BF16),
        compiler_params=_cparams("arbitrary", "arbitrary", "arbitrary"),
        name="flash_attention",
    )(qi_tab, ki_tab, zqkv, zqkv, zqkv, fcum, fcum_t)


def _decode_kernel(pt_ref, q_ref, kn_ref, vn_ref, ln_ref, lp_hbm, k_hbm, v_hbm, o_ref,
                   lp_buf, k_buf, v_buf, sem, m_ref, l_ref, acc_ref, car_ref,
                   *, layer, scale, n_seq, n_pages, group):
    b = pl.program_id(0)
    i = pl.program_id(1)
    n_steps = n_pages // group
    step = b * n_steps + i
    slot = lax.rem(step, 2)

    def page_copies(seq, st, sl):
        out = []
        for g in range(group):
            pg = pt_ref[seq * n_pages + (n_pages - 1 - (st * group + g))]
            out.append(pltpu.make_async_copy(lp_hbm.at[layer, pg], lp_buf.at[sl, g], sem.at[sl]))
            out.append(pltpu.make_async_copy(k_hbm.at[layer, pg], k_buf.at[sl, g], sem.at[sl]))
            out.append(pltpu.make_async_copy(v_hbm.at[layer, pg], v_buf.at[sl, g], sem.at[sl]))
        return out

    @pl.when(step == 0)
    def _():
        for cp in page_copies(b, i, slot):
            cp.start()

    @pl.when(step + 1 < n_seq * n_steps)
    def _():
        wrap = i + 1 == n_steps
        for cp in page_copies(jnp.where(wrap, b + 1, b), jnp.where(wrap, 0, i + 1), 1 - slot):
            cp.start()

    for cp in page_copies(b, i, slot):
        cp.wait()
    lp_refs = [lp_buf.at[slot, g] for g in range(group)]
    k_refs = [k_buf.at[slot, g] for g in range(group)]
    v_refs = [v_buf.at[slot, g] for g in range(group)]

    @pl.when(i == 0)
    def _():
        m_ref[...] = jnp.full_like(m_ref, -jnp.inf)
        l_ref[...] = jnp.zeros_like(l_ref)
        acc_ref[...] = jnp.zeros_like(acc_ref)
        car_ref[...] = ln_ref[...]

    q = q_ref[...]
    qs = q * scale
    page = lp_refs[0].shape[0]
    r = lax.broadcasted_iota(I32, (page, page), 0)
    c = lax.broadcasted_iota(I32, (page, page), 1)
    after = jnp.where(c > r, 1.0, 0.0).astype(F32)
    carry = car_ref[...]
    scores = []
    for g in range(group):
        lp = lp_refs[g][...]
        suffix = jnp.dot(after, lp, preferred_element_type=F32, precision=lax.Precision.HIGHEST) + carry
        carry = suffix[0:1, :] + lp[0:1, :]
        scores.append(jnp.sum(k_refs[g][...] * qs[None], axis=-1) + suffix)
    car_ref[...] = carry
    m_prev = m_ref[...]
    m_new = m_prev
    for s in scores:
        m_new = jnp.maximum(m_new, jnp.max(s, axis=0, keepdims=True))
    alpha = jnp.exp(m_prev - m_new)
    l_new = alpha * l_ref[...]
    acc = alpha.reshape(-1, 1) * acc_ref[...]
    for g in range(group):
        p = jnp.exp(scores[g] - m_new)
        l_new = l_new + jnp.sum(p, axis=0, keepdims=True)
        acc = acc + jnp.sum(p[:, :, None] * v_refs[g][...], axis=0)
    l_ref[...] = l_new
    acc_ref[...] = acc
    m_ref[...] = m_new

    @pl.when(i == n_steps - 1)
    def _():
        s_new = jnp.sum(qs * kn_ref[...], axis=-1, keepdims=True)
        m_col = m_new.reshape(-1, 1)
        m_fin = jnp.maximum(m_col, s_new)
        a = jnp.exp(m_col - m_fin)
        p_new = jnp.exp(s_new - m_fin)
        o_ref[...] = (a * acc + p_new * vn_ref[...]) / (a * l_new.reshape(-1, 1) + p_new)


def decode_attention(q, k_new, v_new, logf_new, cache_k, cache_v, cache_logf, page_table, layer):
    nb, heads, dh = q.shape
    page = cache_k.shape[2]
    n_pages = page_table.shape[1]
    group = _pick(n_pages, 4)
    n_steps = n_pages // group
    pt = page_table.reshape(-1).astype(I32)
    kern = functools.partial(_decode_kernel, layer=layer, scale=dh ** -0.5, n_seq=nb, n_pages=n_pages, group=group)
    tok = pl.BlockSpec((None, heads, dh), lambda b, i, pt_ref: (b, 0, 0))
    hbm = pl.BlockSpec(memory_space=pl.ANY)
    return pl.pallas_call(
        kern,
        grid_spec=pltpu.PrefetchScalarGridSpec(
            num_scalar_prefetch=1,
            grid=(nb, n_steps),
            in_specs=[tok, tok, tok, pl.BlockSpec((None, 1, heads), lambda b, i, pt_ref: (b, 0, 0)), hbm, hbm, hbm],
            out_specs=tok,
            scratch_shapes=[pltpu.VMEM((2, group, page, heads), F32),
                            pltpu.VMEM((2, group, page, heads, dh), F32),
                            pltpu.VMEM((2, group, page, heads, dh), F32),
                            pltpu.SemaphoreType.DMA((2,)),
                            pltpu.VMEM((1, heads), F32), pltpu.VMEM((1, heads), F32), pltpu.VMEM((heads, dh), F32),
                            pltpu.VMEM((1, heads), F32)],
        ),
        out_shape=jax.ShapeDtypeStruct((nb, heads, dh), F32),
        compiler_params=_cparams("arbitrary", "arbitrary"),
        name="decode_attention",
    )(pt, q, k_new, v_new, logf_new, cache_logf, cache_k, cache_v)


def _conv_kernel(b_ref, c_ref, x_ref, w_ref, prev_ref, y_ref, st_ref, car_ref):
    t = pl.program_id(2)
    v = c_ref[...] * x_ref[...]
    bt = v.shape[0]

    @pl.when(t == 0)
    def _():
        car_ref[...] = prev_ref[0]

    p0 = car_ref[0:1, :]
    p1 = car_ref[1:2, :]
    row = lax.broadcasted_iota(I32, v.shape, 0)
    r1 = jnp.where(row == 0, p1, pltpu.roll(v, 1, axis=0))
    r2 = jnp.where(row == 0, p0, jnp.where(row == 1, p1, pltpu.roll(v, 2, axis=0)))
    w = w_ref[...]
    y = w[0:1, :] * r2 + w[1:2, :] * r1 + w[2:3, :] * v
    y_ref[...] = (b_ref[...] * y).astype(y_ref.dtype)
    tail = v[bt - 2:bt, :]
    car_ref[...] = tail
    st_ref[0] = tail


def short_conv_prompt(zrest, col_b, conv_w_l, prev, nb, t, cw):
    n = nb * t
    bt = _pick(t, 512)
    bc = _pick(cw, 512)
    nt = t // bt
    o = col_b // bc
    per = cw // bc
    zspec = lambda k: pl.BlockSpec((bt, bc), lambda b, c, i: (b * nt + i, o + k * per + c))
    return pl.pallas_call(
        _conv_kernel,
        grid=(nb, per, nt),
        in_specs=[zspec(0), zspec(1), zspec(2),
                  pl.BlockSpec((3, bc), lambda b, c, i: (0, c)),
                  pl.BlockSpec((1, 2, bc), lambda b, c, i: (b, 0, c))],
        out_specs=[pl.BlockSpec((bt, bc), lambda b, c, i: (b * nt + i, c)),
                   pl.BlockSpec((1, 2, bc), lambda b, c, i: (b, 0, c))],
        out_shape=[jax.ShapeDtypeStruct((n, cw), BF16), jax.ShapeDtypeStruct((nb, 2, cw), F32)],
        scratch_shapes=[pltpu.VMEM((2, bc), F32)],
        compiler_params=_cparams("arbitrary", "arbitrary", "arbitrary"),
        name="short_conv_prompt",
    )(zrest, zrest, zrest, conv_w_l, prev)


def _conv_step_kernel(b_ref, c_ref, x_ref, w_ref, p0_ref, p1_ref, y_ref, v_ref):
    v = c_ref[...] * x_ref[...]
    w = w_ref[...]
    y = w[0:1, :] * p0_ref[...] + w[1:2, :] * p1_ref[...] + w[2:3, :] * v
    y_ref[...] = (b_ref[...] * y).astype(y_ref.dtype)
    v_ref[...] = v


def short_conv_step(zrest, col_b, conv_w_l, prev0, prev1, cw):
    nb = zrest.shape[0]
    bc = _pick(cw, 512)
    o = col_b // bc
    per = cw // bc
    zspec = lambda k: pl.BlockSpec((nb, bc), lambda c: (0, o + k * per + c))
    vec = pl.BlockSpec((nb, bc), lambda c: (0, c))
    return pl.pallas_call(
        _conv_step_kernel,
        grid=(per,),
        in_specs=[zspec(0), zspec(1), zspec(2), pl.BlockSpec((3, bc), lambda c: (0, c)), vec, vec],
        out_specs=[vec, vec],
        out_shape=[jax.ShapeDtypeStruct((nb, cw), BF16), jax.ShapeDtypeStruct((nb, cw), F32)],
        compiler_params=_cparams("arbitrary"),
        name="short_conv_step",
    )(zrest, zrest, zrest, conv_w_l, prev0, prev1)


def _ssm_weights_kernel(lre_ref, lim_ref, ldt_ref, bre_ref, bim_ref, cre_ref, cim_ref,
                        t_ref, s_ref, o_ref, al_ref, a1_ref, *, chunk):
    ns = lre_ref.shape[1]
    lr = lre_ref[...]
    li = lim_ref[...]
    dt = jnp.exp(ldt_ref[...])
    mag = jnp.exp(lr * dt)
    ang = li * dt
    ar = mag * jnp.cos(ang)
    ai = mag * jnp.sin(ang)
    den = lr * lr + li * li
    nr = ar - 1.0
    fr = (nr * lr + ai * li) / den
    fi = (ai * lr - nr * li) / den
    bre = bre_ref[...]
    bim = bim_ref[...]
    bbr = fr * bre - fi * bim
    bbi = fr * bim + fi * bre
    cre = cre_ref[...]
    cim = cim_ref[...]
    cfull = jnp.concatenate([cre, -cim], axis=1)
    pows = []
    pr = jnp.ones_like(ar)
    pi = jnp.zeros_like(ar)
    for _ in range(chunk + 1):
        pows.append((pr, pi))
        pr, pi = pr * ar - pi * ai, pr * ai + pi * ar
    a1_ref[...] = jnp.concatenate([ar, ai], axis=1)
    al_ref[...] = jnp.concatenate(list(pows[chunk]), axis=1)
    taps = [None] * chunk
    for j in range(chunk):
        qr, qi = pows[chunk - 1 - j]
        blk = jnp.concatenate([qr * bbr - qi * bbi, qr * bbi + qi * bbr], axis=1)
        s_ref[j * LANES:(j + 1) * LANES, :] = blk.astype(s_ref.dtype)
        taps[chunk - 1 - j] = lax.dot_general(blk, cfull, NT_DIMS, preferred_element_type=F32,
                                              precision=lax.Precision.HIGHEST)
    zero = jnp.zeros((LANES, LANES), F32)
    for j in range(chunk):
        for i in range(chunk):
            t_ref[j * LANES:(j + 1) * LANES, i * LANES:(i + 1) * LANES] = (
                taps[i - j] if i >= j else zero).astype(t_ref.dtype)
    for i in range(chunk):
        qr, qi = pows[i + 1]
        o_ref[i * LANES:(i + 1) * LANES, :] = jnp.concatenate(
            [cre * qr - cim * qi, -cre * qi - cim * qr], axis=1).astype(o_ref.dtype)


def ssm_weights(lam_re, lam_im, log_dt, b_re, b_im, c_re, c_im, chunk):
    g, p = lam_re.shape
    gc = b_re.shape[-1]
    gpt = LANES // gc
    nt = g // gpt
    ns = gpt * p
    eye = jnp.eye(gpt, dtype=F32)
    tile = lambda x: x.reshape(nt, 1, ns)
    ldt = tile(jnp.broadcast_to(log_dt[:, None], (g, p)))
    bdiag = lambda b: jnp.einsum('tgpc,gh->thcgp', b.reshape(nt, gpt, p, gc), eye).reshape(nt, LANES, ns)
    cdiag = lambda c: jnp.einsum('tgcp,gh->thcgp', c.reshape(nt, gpt, gc, p), eye).reshape(nt, LANES, ns)
    row = pl.BlockSpec((None, 1, ns), lambda j: (j, 0, 0))
    mat = pl.BlockSpec((None, LANES, ns), lambda j: (j, 0, 0))
    lc = chunk * LANES
    kern = functools.partial(_ssm_weights_kernel, chunk=chunk)
    return pl.pallas_call(
        kern,
        grid=(nt,),
        in_specs=[row, row, row, mat, mat, mat, mat],
        out_specs=[pl.BlockSpec((None, lc, lc), lambda j: (j, 0, 0)),
                   pl.BlockSpec((None, lc, 2 * ns), lambda j: (j, 0, 0)),
                   pl.BlockSpec((None, lc, 2 * ns), lambda j: (j, 0, 0)),
                   pl.BlockSpec((None, 1, 2 * ns), lambda j: (j, 0, 0)),
                   pl.BlockSpec((None, 1, 2 * ns), lambda j: (j, 0, 0))],
        out_shape=[jax.ShapeDtypeStruct((nt, lc, lc), BF16),
                   jax.ShapeDtypeStruct((nt, lc, 2 * ns), BF16),
                   jax.ShapeDtypeStruct((nt, lc, 2 * ns), BF16),
                   jax.ShapeDtypeStruct((nt, 1, 2 * ns), F32),
                   jax.ShapeDtypeStruct((nt, 1, 2 * ns), F32)],
        compiler_params=_cparams("arbitrary"),
        name="ssm_weights",
    )(tile(lam_re), tile(lam_im), ldt, bdiag(b_re), bdiag(b_im), cdiag(c_re), cdiag(c_im))


def _ssm_kernel(u_ref, h0_ref, t_ref, s_ref, o_ref, al_ref, d_ref, z_ref, hf_ref, ucat_ref, sloc_ref, sprev_ref,
                *, chunk):
    nb, t, _ = u_ref.shape
    tc = t // chunk
    ns = al_ref.shape[1] // 2
    for b in range(nb):
        for j in range(chunk):
            ucat_ref[b * tc:(b + 1) * tc, j * LANES:(j + 1) * LANES] = u_ref[b, pl.ds(j, tc, stride=chunk), :]
    nc = ns // LANES
    for b in range(nb):
        rows = slice(b * tc, (b + 1) * tc)
        sl = jnp.dot(ucat_ref[rows, :].astype(BF16), s_ref[...], preferred_element_type=F32)
        for c in range(2 * nc):
            sloc_ref[c, rows, :] = sl[:, c * LANES:(c + 1) * LANES]

    def step(k, h):
        at = pl.ds(k, nb, stride=tc)
        new = [None] * (2 * nc)
        for c in range(nc):
            lanes = slice(c * LANES, (c + 1) * LANES)
            alr = al_ref[:, lanes]
            ali = al_ref[:, ns + c * LANES:ns + (c + 1) * LANES]
            hr = h[c]
            hi = h[nc + c]
            sprev_ref[c, at, :] = hr
            sprev_ref[nc + c, at, :] = hi
            new[c] = alr * hr - ali * hi + sloc_ref[c, at, :]
            new[nc + c] = alr * hi + ali * hr + sloc_ref[nc + c, at, :]
        return tuple(new)

    h0 = h0_ref[...]
    hfin = lax.fori_loop(0, tc, step, tuple(h0[:, c * LANES:(c + 1) * LANES] for c in range(2 * nc)))
    hf_ref[...] = jnp.concatenate(hfin, axis=1)
    for b in range(nb):
        rows = slice(b * tc, (b + 1) * tc)
        uc = ucat_ref[rows, :]
        sp = jnp.concatenate([sprev_ref[c, rows, :] for c in range(2 * nc)], axis=1)
        y = jnp.dot(uc.astype(BF16), t_ref[...], preferred_element_type=F32)
        y = y + lax.dot_general(sp.astype(BF16), o_ref[...], NT_DIMS, preferred_element_type=F32)
        z = jax.nn.gelu(y + d_ref[...] * uc)
        for i in range(chunk):
            z_ref[b, pl.ds(i, tc, stride=chunk), :] = z[:, i * LANES:(i + 1) * LANES]


def ssm_prompt(zrest3, col_u, wts, h0cat, d_cat, chunk):
    tmat, smat, omat, al, _ = wts
    nb, t, _ = zrest3.shape
    nt, lc, ns2 = smat.shape
    tc = t // chunk
    kern = functools.partial(_ssm_kernel, chunk=chunk)
    return pl.pallas_call(
        kern,
        grid=(nt,),
        in_specs=[pl.BlockSpec((nb, t, LANES), lambda j: (0, 0, col_u // LANES + j)),
                  pl.BlockSpec((None, nb, ns2), lambda j: (j, 0, 0)),
                  pl.BlockSpec((None, lc, lc), lambda j: (j, 0, 0)),
                  pl.BlockSpec((None, lc, ns2), lambda j: (j, 0, 0)),
                  pl.BlockSpec((None, lc, ns2), lambda j: (j, 0, 0)),
                  pl.BlockSpec((None, 1, ns2), lambda j: (j, 0, 0)),
                  pl.BlockSpec((None, 1, lc), lambda j: (j, 0, 0))],
        out_specs=[pl.BlockSpec((nb, t, LANES), lambda j: (0, 0, j)),
                   pl.BlockSpec((None, nb, ns2), lambda j: (j, 0, 0))],
        out_shape=[jax.ShapeDtypeStruct((nb, t, nt * LANES), F32), jax.ShapeDtypeStruct((nt, nb, ns2), F32)],
        scratch_shapes=[pltpu.VMEM((nb * tc, lc), F32), pltpu.VMEM((ns2 // LANES, nb * tc, LANES), F32),
                        pltpu.VMEM((ns2 // LANES, nb * tc, LANES), F32)],
        compiler_params=_cparams("arbitrary"),
        name="ssm_prompt",
    )(zrest3, h0cat, tmat, smat, omat, al, d_cat)


def _ssm_step_kernel(u_ref, h0_ref, s_ref, o_ref, a1_ref, d_ref, z_ref, h_ref, *, chunk):
    ns = a1_ref.shape[1] // 2
    u = u_ref[...]
    x = jnp.dot(u.astype(BF16), s_ref[(chunk - 1) * LANES:chunk * LANES, :], preferred_element_type=F32)
    ar = a1_ref[:, :ns]
    ai = a1_ref[:, ns:]
    h0 = h0_ref[...]
    hr = ar * h0[:, :ns] - ai * h0[:, ns:] + x[:, :ns]
    hi = ar * h0[:, ns:] + ai * h0[:, :ns] + x[:, ns:]
    h = jnp.concatenate([hr, hi], axis=1)
    h_ref[...] = h
    y = lax.dot_general(h.astype(BF16), o_ref[...], NT_DIMS, preferred_element_type=F32)
    z_ref[...] = jax.nn.gelu(y + d_ref[...] * u)


def ssm_step(zrest, col_u, wts, cmat, h0cat, d_t, chunk):
    _, smat, _, _, a1 = wts
    nb = zrest.shape[0]
    nt, lc, ns2 = smat.shape
    kern = functools.partial(_ssm_step_kernel, chunk=chunk)
    return pl.pallas_call(
        kern,
        grid=(nt,),
        in_specs=[pl.BlockSpec((nb, LANES), lambda j: (0, col_u // LANES + j)),
                  pl.BlockSpec((None, nb, ns2), lambda j: (j, 0, 0)),
                  pl.BlockSpec((None, lc, ns2), lambda j: (j, 0, 0)),
                  pl.BlockSpec((None, LANES, ns2), lambda j: (j, 0, 0)),
                  pl.BlockSpec((None, 1, ns2), lambda j: (j, 0, 0)),
                  pl.BlockSpec((None, 1, LANES), lambda j: (j, 0, 0))],
        out_specs=[pl.BlockSpec((nb, LANES), lambda j: (0, j)),
                   pl.BlockSpec((None, nb, ns2), lambda j: (j, 0, 0))],
        out_shape=[jax.ShapeDtypeStruct((nb, nt * LANES), F32), jax.ShapeDtypeStruct((nt, nb, ns2), F32)],
        compiler_params=_cparams("arbitrary"),
        name="ssm_step",
    )(zrest, h0cat, smat, cmat, a1, d_t)


def _glu_kernel(a_ref, zt_ref, w_ref, o_ref, wbf_ref):
    @pl.when(pl.program_id(1) == 0)
    def _():
        wbf_ref[...] = w_ref[...].astype(BF16)

    acc = jnp.dot(a_ref[...].astype(BF16), wbf_ref[...], preferred_element_type=F32)
    o_ref[...] = (zt_ref[...] * jax.nn.sigmoid(acc)).astype(o_ref.dtype)


def glu(z, w3, layer):
    m, k = z.shape
    bm = _pick(m, 1024)
    bn = _pick(k, 512)
    return pl.pallas_call(
        _glu_kernel,
        grid=(k // bn, m // bm),
        in_specs=[pl.BlockSpec((bm, k), lambda j, i: (i, 0)),
                  pl.BlockSpec((bm, bn), lambda j, i: (i, j)),
                  pl.BlockSpec((None, k, bn), lambda j, i: (layer, 0, j))],
        out_specs=pl.BlockSpec((bm, bn), lambda j, i: (i, j)),
        out_shape=jax.ShapeDtypeStruct((m, k), BF16),
        scratch_shapes=[pltpu.VMEM((k, bn), BF16)],
        compiler_params=_cparams("arbitrary", "arbitrary"),
        name="glu",
    )(z, z, w3)


def _merge_kernel(ya_ref, yc_ref, ys_ref, ga_ref, gc_ref, gs_ref, wa_ref, wc_ref, ws_ref, o_ref,
                  wa_bf, wc_bf, ws_bf):
    @pl.when(pl.program_id(1) == 0)
    def _():
        wa_bf[...] = wa_ref[...].astype(BF16)
        wc_bf[...] = wc_ref[...].astype(BF16)
        ws_bf[...] = ws_ref[...].astype(BF16)

    acc = jax.nn.sigmoid(ga_ref[...]) * jnp.dot(ya_ref[...], wa_bf[...], preferred_element_type=F32)
    acc = acc + jax.nn.sigmoid(gc_ref[...]) * jnp.dot(yc_ref[...], wc_bf[...], preferred_element_type=F32)
    acc = acc + jax.nn.sigmoid(gs_ref[...]) * jnp.dot(ys_ref[...], ws_bf[...], preferred_element_type=F32)
    o_ref[...] = acc.astype(o_ref.dtype)


def merge_branches(ya, yc, ys, zrest, col_g, w_a, w_c, w_s, layer, d):
    m = ya.shape[0]
    bm = _pick(m, 512)
    bn = _pick(d, 512)
    og = col_g // bn
    per = d // bn
    yspec = lambda y: pl.BlockSpec((bm, y.shape[1]), lambda j, i: (i, 0))
    gspec = lambda k: pl.BlockSpec((bm, bn), lambda j, i: (i, og + k * per + j))
    wspec = lambda w: pl.BlockSpec((None, w.shape[1], bn), lambda j, i: (layer, 0, j))
    return pl.pallas_call(
        _merge_kernel,
        grid=(per, m // bm),
        in_specs=[yspec(ya), yspec(yc), yspec(ys), gspec(0), gspec(1), gspec(2), wspec(w_a), wspec(w_c), wspec(w_s)],
        out_specs=pl.BlockSpec((bm, bn), lambda j, i: (i, j)),
        out_shape=jax.ShapeDtypeStruct((m, d), BF16),
        scratch_shapes=[pltpu.VMEM((w_a.shape[1], bn), BF16), pltpu.VMEM((w_c.shape[1], bn), BF16),
                        pltpu.VMEM((w_s.shape[1], bn), BF16)],
        compiler_params=_cparams("arbitrary", "arbitrary"),
        name="merge_branches",
    )(ya, yc, ys, zrest, zrest, zrest, w_a, w_c, w_s)


def _gather_lane_tiles(ref, lead=()):
    return jnp.concatenate([ref[lead + (slice(None), c, slice(None))] for c in range(ref.shape[-2])], axis=1)


def _scatter_lane_tiles(ref, val, lead=()):
    for c in range(ref.shape[-2]):
        ref[lead + (slice(None), c, slice(None))] = val[:, c * LANES:(c + 1) * LANES]


def _layer_norm(v, g, b):
    mu = jnp.mean(v, axis=-1, keepdims=True)
    var = jnp.mean(jnp.square(v - mu), axis=-1, keepdims=True)
    return (v - mu) * lax.rsqrt(var + LN_EPS) * g + b


def _ln_route_kernel(x_ref, y_ref, gate_ref, g_ref, b_ref, sc_ref, sh_ref, wr_ref, br_ref,
                     x1_ref, tok_ref, eidx_ref, wts_ref, *, alpha, n_groups, per_group):
    x1 = _layer_norm(alpha * x_ref[0] + (1.0 + gate_ref[0]) * y_ref[0], g_ref[...], b_ref[...])
    x1_ref[0] = x1
    tok = x1 * (1.0 + sc_ref[0]) + sh_ref[0]
    _scatter_lane_tiles(tok_ref, tok, (0,))
    logit = jnp.dot(tok.astype(BF16), wr_ref[...].astype(BF16), preferred_element_type=F32) + br_ref[...]
    lane_i = lax.broadcasted_iota(I32, logit.shape, 1)
    lane = lane_i.astype(F32)
    big = float(LANES)
    neg = -jnp.inf
    gl = jnp.where(lane < n_groups, logit, neg)
    gmax = jnp.max(gl, axis=1, keepdims=True)
    gidx = jnp.min(jnp.where(gl == gmax, lane, big), axis=1, keepdims=True)
    gprob = 1.0 / jnp.sum(jnp.exp(gl - gmax), axis=1, keepdims=True)
    lo = n_groups + gidx * per_group
    el = jnp.where((lane >= lo) & (lane < lo + per_group), logit, neg)
    t1 = jnp.max(el, axis=1, keepdims=True)
    i1 = jnp.min(jnp.where(el == t1, lane, big), axis=1, keepdims=True)
    el2 = jnp.where(lane == i1, neg, el)
    t2 = jnp.max(el2, axis=1, keepdims=True)
    i2 = jnp.min(jnp.where(el2 == t2, lane, big), axis=1, keepdims=True)
    e2 = jnp.exp(t2 - t1)
    w1 = gprob / (1.0 + e2)
    w2 = gprob * e2 / (1.0 + e2)
    eidx_ref[0] = jnp.where(lane_i == 0, i1 - n_groups, jnp.where(lane_i == 1, i2 - n_groups, 0.0)).astype(I32)
    wts_ref[0] = jnp.where(lane_i == 0, w1, jnp.where(lane_i == 1, w2, 0.0))


def ln_route(x3, y3, gate, ln_g, ln_b, sc, sh, w_router, b_router, alpha, n_groups, per_group):
    nb, t, d = x3.shape
    bt = _pick(t, 256)
    blk = pl.BlockSpec((1, bt, d), lambda b, i: (b, i, 0))
    row = pl.BlockSpec((1, d), lambda b, i: (0, 0))
    sel = pl.BlockSpec((1, bt, LANES), lambda b, i: (b, i, 0))
    tiles = pl.BlockSpec((1, bt, d // LANES, LANES), lambda b, i: (b, i, 0, 0))
    kern = functools.partial(_ln_route_kernel, alpha=alpha, n_groups=n_groups, per_group=per_group)
    return pl.pallas_call(
        kern,
        grid=(nb, t // bt),
        in_specs=[blk, blk, _mod_spec(gate, bt), row, row, _mod_spec(sc, bt), _mod_spec(sh, bt),
                  pl.BlockSpec((d, LANES), lambda b, i: (0, 0)), pl.BlockSpec((1, LANES), lambda b, i: (0, 0))],
        out_specs=[blk, tiles, sel, sel],
        out_shape=[jax.ShapeDtypeStruct((nb, t, d), F32), jax.ShapeDtypeStruct((nb, t, d // LANES, LANES), F32),
                   jax.ShapeDtypeStruct((nb, t, LANES), I32), jax.ShapeDtypeStruct((nb, t, LANES), F32)],
        compiler_params=_cparams("arbitrary", "arbitrary"),
        name="ln_route",
    )(x3, y3, gate, ln_g, ln_b, sc, sh, w_router, b_router)


def _ln_combine_kernel(x_ref, y0_ref, y1_ref, wts_ref, gate_ref, g_ref, b_ref, o_ref, *, alpha):
    w = wts_ref[0]
    ffn = w[:, 0:1] * _gather_lane_tiles(y0_ref, (0,)) + w[:, 1:2] * _gather_lane_tiles(y1_ref, (0,))
    o_ref[0] = _layer_norm(alpha * x_ref[0] + (1.0 + gate_ref[0]) * ffn, g_ref[...], b_ref[...])


def ln_combine(x3, y_assign, wts, gate, ln_g, ln_b, alpha):
    nb, t, d = x3.shape
    bt = _pick(t, 256)
    blk = pl.BlockSpec((1, bt, d), lambda b, i: (b, i, 0))
    row = pl.BlockSpec((1, d), lambda b, i: (0, 0))
    ysp = lambda k: pl.BlockSpec((None, 1, bt, d // LANES, LANES), lambda b, i: (k, b, i, 0, 0))
    kern = functools.partial(_ln_combine_kernel, alpha=alpha)
    return pl.pallas_call(
        kern,
        grid=(nb, t // bt),
        in_specs=[blk, ysp(0), ysp(1), pl.BlockSpec((1, bt, LANES), lambda b, i: (b, i, 0)),
                  _mod_spec(gate, bt), row, row],
        out_specs=blk,
        out_shape=jax.ShapeDtypeStruct((nb, t, d), F32),
        compiler_params=_cparams("arbitrary", "arbitrary"),
        name="ln_combine",
    )(x3, y_assign, y_assign, wts, gate, ln_g, ln_b)


def _rank_kernel(e_ref, rank_ref, cnt_ref, car_ref):
    first = (pl.program_id(0) == 0) & (pl.program_id(1) == 0)

    @pl.when(first)
    def _():
        car_ref[...] = jnp.zeros_like(car_ref)

    e = e_ref[...]
    rb = e.shape[1]
    ex = lax.broadcasted_iota(I32, (LANES, rb), 0)
    onehot = jnp.where(ex == e, 1.0, 0.0).astype(BF16)
    r = lax.broadcasted_iota(I32, (rb, rb), 0)
    c = lax.broadcasted_iota(I32, (rb, rb), 1)
    upto = jnp.where(r <= c, 1.0, 0.0).astype(BF16)
    cum = jnp.dot(onehot, upto, preferred_element_type=F32)
    oh = onehot.astype(F32)
    rank = jnp.sum(oh * (cum - 1.0 + car_ref[...]), axis=0, keepdims=True)
    rank_ref[...] = rank.astype(I32)
    car_ref[...] = car_ref[...] + jnp.sum(oh, axis=1, keepdims=True)
    cnt_ref[...] = jnp.broadcast_to(car_ref[...], cnt_ref.shape).astype(I32)


def expert_ranks(e_rows):
    two, nblk, _, rb = e_rows.shape
    return pl.pallas_call(
        _rank_kernel,
        grid=(two, nblk),
        in_specs=[pl.BlockSpec((None, None, 1, rb), lambda k, i: (k, i, 0, 0))],
        out_specs=[pl.BlockSpec((None, None, 1, rb), lambda k, i: (k, i, 0, 0)),
                   pl.BlockSpec((LANES, LANES), lambda k, i: (0, 0))],
        out_shape=[jax.ShapeDtypeStruct(e_rows.shape, I32), jax.ShapeDtypeStruct((LANES, LANES), I32)],
        scratch_shapes=[pltpu.VMEM((LANES, 1), F32)],
        compiler_params=_cparams("arbitrary", "arbitrary"),
        name="expert_ranks",
    )(e_rows)


def _expert_kernel(dst_ref, be_ref, nu_ref, tok_hbm, wg_ref, wu_ref, wd_ref, spill_in, y_hbm, spill_hbm,
                   xbuf, ybuf, gsem, ssem, *, rows, nblk, n_tok):
    del spill_in
    del be_ref
    i = pl.program_id(0)
    slot = lax.rem(i, 2)
    n_used = nu_ref[0]
    n_asg = 2 * n_tok

    def fetch(blk, sl):
        @pl.when(blk < n_used)
        def _():
            def body(u, c):
                a = dst_ref[blk * rows + u]
                pltpu.make_async_copy(tok_hbm.at[pl.ds(lax.rem(a, n_tok), 1)], xbuf.at[sl, pl.ds(u, 1)],
                                      gsem.at[sl]).start()
                return c

            lax.fori_loop(0, rows, body, 0, unroll=8)

    def store(blk, sl):
        @pl.when(blk < n_used)
        def _():
            def body(u, c):
                a = dst_ref[blk * rows + u]
                src = ybuf.at[sl, pl.ds(u, 1)]

                @pl.when(a < n_asg)
                def _():
                    pltpu.make_async_copy(src, y_hbm.at[pl.ds(a, 1)], ssem.at[sl]).start()

                @pl.when(a >= n_asg)
                def _():
                    pltpu.make_async_copy(src, spill_hbm.at[pl.ds(a - n_asg, 1)], ssem.at[sl]).start()

                return c

            lax.fori_loop(0, rows, body, 0, unroll=8)

    def wait_block(blk, buf, sem, sl):
        @pl.when(blk < n_used)
        def _():
            pltpu.make_async_copy(buf.at[sl], buf.at[sl], sem.at[sl]).wait()

    @pl.when(i == 0)
    def _():
        fetch(0, 0)

    @pl.when(i + 1 < nblk)
    def _():
        fetch(i + 1, 1 - slot)

    wait_block(i, xbuf, gsem, slot)

    @pl.when(i >= 2)
    def _():
        wait_block(i - 2, ybuf, ssem, slot)

    @pl.when(i < n_used)
    def _():
        xb = _gather_lane_tiles(xbuf.at[slot]).astype(BF16)
        g = jnp.dot(xb, wg_ref[...], preferred_element_type=F32)
        u = jnp.dot(xb, wu_ref[...], preferred_element_type=F32)
        h = (g * jax.nn.sigmoid(g) * u).astype(BF16)
        _scatter_lane_tiles(ybuf.at[slot], jnp.dot(h, wd_ref[...], preferred_element_type=F32))

    store(i, slot)

    @pl.when(i == nblk - 1)
    def _():
        if nblk >= 2:
            wait_block(i - 1, ybuf, ssem, 1 - slot)
        wait_block(i, ybuf, ssem, slot)


def expert_blocks(tok, dst_sorted, blk_e, n_used, wg, wu, wd, layer, rows):
    n, nc, _ = tok.shape
    d = nc * LANES
    hid = wg.shape[-1]
    nblk = dst_sorted.shape[0] // rows
    live = lambda i, nu: jnp.minimum(i, nu[0] - 1)
    hbm = pl.BlockSpec(memory_space=pl.ANY)
    kern = functools.partial(_expert_kernel, rows=rows, nblk=nblk, n_tok=n)
    spill = jnp.zeros((2 * rows, nc, LANES), F32)
    y, _ = pl.pallas_call(
        kern,
        grid_spec=pltpu.PrefetchScalarGridSpec(
            num_scalar_prefetch=3,
            grid=(nblk,),
            in_specs=[hbm,
                      pl.BlockSpec((None, None, d, hid), lambda i, ds, be, nu: (layer, be[live(i, nu)], 0, 0)),
                      pl.BlockSpec((None, None, d, hid), lambda i, ds, be, nu: (layer, be[live(i, nu)], 0, 0)),
                      pl.BlockSpec((None, None, hid, d), lambda i, ds, be, nu: (layer, be[live(i, nu)], 0, 0)),
                      hbm],
            out_specs=[hbm, hbm],
            scratch_shapes=[pltpu.VMEM((2, rows, nc, LANES), F32), pltpu.VMEM((2, rows, nc, LANES), F32),
                            pltpu.SemaphoreType.DMA((2,)), pltpu.SemaphoreType.DMA((2,))],
        ),
        out_shape=[jax.ShapeDtypeStruct((2 * n, nc, LANES), F32), jax.ShapeDtypeStruct(spill.shape, F32)],
        input_output_aliases={7: 1},
        compiler_params=_cparams("arbitrary"),
        name="expert_blocks",
    )(dst_sorted, blk_e, n_used, tok, wg, wu, wd, spill)
    return y


def hierarchical_moe(tok3, eidx, wg, wu, wd, layer, rows):
    nb, t, nc, _ = tok3.shape
    n = nb * t
    n_exp = wg.shape[1]
    e2 = eidx.reshape(n, LANES)[:, :2].T
    if n >= LANES:
        rb = _pick(n, 512)
        rank, cnt = expert_ranks(e2.reshape(2, n // rb, 1, rb))
        rank = rank.reshape(-1)
        counts = cnt[:n_exp, 0]
    else:
        ef = e2.reshape(-1)
        ar = jnp.arange(2 * n)
        rank = jnp.sum((ef[:, None] == ef[None, :]) & (ar[None, :] < ar[:, None]), axis=1).astype(I32)
        counts = jnp.sum(ef[:, None] == jnp.arange(n_exp)[None, :], axis=0).astype(I32)
    ef = e2.reshape(-1)
    nblk = min(-(-(2 * n + n_exp * (rows - 1)) // rows), 2 * n)
    padded = (counts + rows - 1) // rows * rows
    pad_end = jnp.cumsum(padded)
    dest = ((pad_end - padded)[ef] + rank).astype(I32)
    r = jnp.arange(nblk * rows, dtype=I32)
    spill_row = 2 * n + (r // rows) % 2 * rows + r % rows
    dst_sorted = spill_row.at[dest].set(jnp.arange(2 * n, dtype=I32))
    blk_e = jnp.minimum(jnp.sum(pad_end[None, :] <= (jnp.arange(nblk) * rows)[:, None], axis=1), n_exp - 1).astype(I32)
    n_used = (pad_end[-1:] // rows).astype(I32)
    y_assign = expert_blocks(tok3.reshape(n, nc, LANES), dst_sorted, blk_e, n_used, wg, wu, wd, layer, rows)
    return y_assign.reshape(2, nb, t, nc, LANES)


def _dims(p):
    d = p['w_o'].shape[-1]
    aw = p['w_br_attn'].shape[1]
    cw = p['w_br_conv'].shape[1]
    sw = p['w_br_ssm'].shape[1]
    return d, aw, cw, sw


def _input_projection(xin, xin_s, p, layer, nb, t, heads):
    d, aw, cw, sw = _dims(p)
    wt = p['w_in_t']
    base = layer * (wt.shape[0] // p['w_o'].shape[0])
    zqkv, zqkv_s = matmul_nt(xin, wt, base, 3 * aw, a2=xin_s)
    zrest, zrest_s = matmul_nt(xin, wt, base + 3 * aw + heads, 3 * cw + sw + 3 * d, a2=xin_s)
    b_pad = jnp.zeros((1, LANES), F32).at[0, :heads].set(p['b_forget'][layer])
    logf, fcum = forget_gate(xin, wt, base + 3 * aw, b_pad, nb, t, True)
    logf_s, _ = forget_gate(xin_s, wt, base + 3 * aw, b_pad, 1, xin_s.shape[0], False)
    return (zqkv, zrest, logf, fcum), (zqkv_s, zrest_s, logf_s)


def _finish_mixer(x3, ya, yc, ys, zrest, p, layer, mods, alpha, n_groups, per_group, w_router, b_router):
    d, aw, cw, sw = _dims(p)
    nb, t, _ = x3.shape
    merged = merge_branches(ya, yc, ys, zrest, 3 * cw + sw, p['w_br_attn'], p['w_br_conv'], p['w_br_ssm'], layer, d)
    mix = matmul(merged, p['w_o'], layer)
    sh1, sc1, g1, sh2, sc2, g2 = mods
    return ln_route(x3, mix.reshape(nb, t, d), g1, p['ln1_g'][layer][None], p['ln1_b'][layer][None], sc2, sh2,
                    w_router, b_router, alpha, n_groups, per_group)


def _ssm_tiles(state, nt):
    nb = state.shape[0]
    return state.reshape(nb, nt, -1).transpose(1, 0, 2)


def _ssm_untile(h, nb, g, p):
    return h.transpose(1, 0, 2).reshape(nb, g, p)


def kernel(x_prompt, x_sample, c_prompt, c_sample, cache_k, cache_v, cache_logf, page_table, state_conv, state_ssm_re, state_ssm_im, w_ada, b_ada, ln1_g, ln1_b, ln2_g, ln2_b, w_in, b_forget, conv_w, ssm_lambda_re, ssm_lambda_im, ssm_log_dt, ssm_b_re, ssm_b_im, ssm_c_re, ssm_c_im, ssm_d, ssm_w_glu, w_br_attn, w_br_conv, w_br_ssm, w_o, router_w_group, router_b_group, router_w_expert, router_b_expert, moe_w_gate, moe_w_up, moe_w_down):
    p = dict(b_forget=b_forget, w_br_attn=w_br_attn, w_br_conv=w_br_conv, w_br_ssm=w_br_ssm, w_o=w_o,
             ln1_g=ln1_g, ln1_b=ln1_b, ln2_g=ln2_g, ln2_b=ln2_b)
    depth = w_in.shape[0]
    nbp, t, d = x_prompt.shape
    nbs, ts, _ = x_sample.shape
    assert ts == 1
    heads, dh = cache_k.shape[3], cache_k.shape[4]
    aw = heads * dh
    cw = conv_w.shape[-1]
    sw = ssm_d.shape[-1]
    g, pst = ssm_lambda_re.shape[1], ssm_lambda_re.shape[2]
    gc = ssm_b_re.shape[-1]
    nt = g * gc // LANES
    ns = (LANES // gc) * pst
    n_groups = router_w_group.shape[-1]
    n_exp = router_w_expert.shape[-1]
    per_group = n_exp // n_groups
    alpha = (2 * depth) ** 0.25
    chunk = SSM_CHUNK
    col_conv = 0
    col_u = 3 * cw

    r = nbp + nbs
    rpad = -(-r // 8) * 8
    c_all = jnp.zeros((rpad, d), F32).at[:nbp].set(c_prompt).at[nbp:r].set(c_sample)
    mod_all = ada_modulation(c_all, w_ada, b_ada)

    assert w_in.shape[2] % 8 == 0 and (3 * aw + heads) % 8 == 0
    p['w_in_t'] = jnp.swapaxes(w_in, 1, 2).reshape(depth * w_in.shape[2], d)

    wg_bf = moe_w_gate.astype(BF16)
    wu_bf = moe_w_up.astype(BF16)
    wd_bf = moe_w_down.astype(BF16)

    xp = x_prompt
    xs = x_sample.reshape(1, nbs, d)
    outs = {k: [] for k in ('kp', 'vp', 'fp', 'ks', 'vs', 'fs', 'cp', 'cs', 'srp', 'sip', 'srs', 'sis')}
    zero_conv = jnp.zeros((nbp, 2, cw), F32)
    zero_h = jnp.zeros((nt, nbp, 2 * ns), F32)
    for l in range(depth):
        mp = [mod_all[l, :nbp, i * d:(i + 1) * d].reshape(nbp, 1, d) for i in range(6)]
        ms = [mod_all[l, nbp:r, i * d:(i + 1) * d].reshape(1, nbs, d) for i in range(6)]
        w_router = jnp.zeros((d, LANES), F32).at[:, :n_groups].set(router_w_group[l]) \
            .at[:, n_groups:n_groups + n_exp].set(router_w_expert[l])
        b_router = jnp.zeros((1, LANES), F32).at[0, :n_groups].set(router_b_group[l]) \
            .at[0, n_groups:n_groups + n_exp].set(router_b_expert[l])
        wts = ssm_weights(ssm_lambda_re[l], ssm_lambda_im[l], ssm_log_dt[l], ssm_b_re[l], ssm_b_im[l],
                          ssm_c_re[l], ssm_c_im[l], chunk)
        d_t = ssm_d[l].reshape(nt, 1, LANES)
        d_cat = jnp.tile(d_t, (1, 1, chunk))
        eye = jnp.eye(LANES // gc, dtype=F32)
        cdiag = lambda c: jnp.einsum('tgcp,gh->thcgp', c.reshape(nt, LANES // gc, gc, pst), eye).reshape(nt, LANES, ns)
        cmat = jnp.concatenate([cdiag(ssm_c_re[l]), -cdiag(ssm_c_im[l])], axis=2).astype(BF16)

        n = nbp * t
        xin = modulate(xp, mp[1], mp[0]).reshape(n, d)
        xin_s = modulate(xs, ms[1], ms[0]).reshape(nbs, d)
        (zqkv, zrest, logf, fcum), (zqkv_s, zrest_s, logf_s) = _input_projection(xin, xin_s, p, l, nbp, t, heads)
        fcum_t = fcum[:, :heads].reshape(nbp, t, heads).transpose(0, 2, 1)
        ya = flash_attention(zqkv, fcum, fcum_t, nbp, t, heads, dh)
        yc, conv_p = short_conv_prompt(zrest, col_conv, conv_w[l], zero_conv, nbp, t, cw)
        zs, hfin = ssm_prompt(zrest.reshape(nbp, t, -1), col_u, wts, zero_h, d_cat, chunk)
        ys = glu(zs.reshape(n, sw), ssm_w_glu, l)
        x1, tok, eidx, wsel = _finish_mixer(xp, ya, yc, ys, zrest, p, l, mp, alpha, n_groups, per_group,
                                            w_router, b_router)
        y_assign = hierarchical_moe(tok, eidx, wg_bf, wu_bf, wd_bf, l, MOE_ROWS)
        xp = ln_combine(x1, y_assign, wsel, mp[5], ln2_g[l][None], ln2_b[l][None], alpha)
        outs['kp'].append(zqkv[:, aw:2 * aw].reshape(nbp, t, heads, dh))
        outs['vp'].append(zqkv[:, 2 * aw:].reshape(nbp, t, heads, dh))
        outs['fp'].append(logf[:, :heads].reshape(nbp, t, heads))
        outs['cp'].append(conv_p)
        outs['srp'].append(_ssm_untile(hfin[:, :, :ns], nbp, g, pst))
        outs['sip'].append(_ssm_untile(hfin[:, :, ns:], nbp, g, pst))

        q_s = zqkv_s[:, :aw].reshape(nbs, heads, dh)
        k_s = zqkv_s[:, aw:2 * aw].reshape(nbs, heads, dh)
        v_s = zqkv_s[:, 2 * aw:].reshape(nbs, heads, dh)
        ya_s = decode_attention(q_s, k_s, v_s, logf_s[:, :heads].reshape(nbs, 1, heads), cache_k, cache_v, cache_logf,
                                page_table, l)
        ya_s = ya_s.reshape(nbs, aw).astype(BF16)
        yc_s, v_row = short_conv_step(zrest_s, col_conv, conv_w[l], state_conv[l, :, 0], state_conv[l, :, 1], cw)
        h0 = jnp.concatenate([_ssm_tiles(state_ssm_re[l], nt), _ssm_tiles(state_ssm_im[l], nt)], axis=2)
        zs_s, h_s = ssm_step(zrest_s, col_u, wts, cmat, h0, d_t, chunk)
        ys_s = glu(zs_s, ssm_w_glu, l)
        x1_s, tok_s, eidx_s, wsel_s = _finish_mixer(xs, ya_s, yc_s, ys_s, zrest_s, p, l, ms, alpha, n_groups,
                                                    per_group, w_router, b_router)
        y_assign_s = hierarchical_moe(tok_s, eidx_s, wg_bf, wu_bf, wd_bf, l, 8)
        xs = ln_combine(x1_s, y_assign_s, wsel_s, ms[5], ln2_g[l][None], ln2_b[l][None], alpha)
        outs['ks'].append(k_s.reshape(nbs, 1, heads, dh))
        outs['vs'].append(v_s.reshape(nbs, 1, heads, dh))
        outs['fs'].append(logf_s[:, :heads].reshape(nbs, 1, heads))
        outs['cs'].append(jnp.stack([state_conv[l, :, 1], v_row], axis=1))
        outs['srs'].append(_ssm_untile(h_s[:, :, :ns], nbs, g, pst))
        outs['sis'].append(_ssm_untile(h_s[:, :, ns:], nbs, g, pst))

    st = lambda k: jnp.stack(outs[k])
    return (xp, xs.reshape(nbs, 1, d), st('kp'), st('vp'), st('fp'), st('ks'), st('vs'), st('fs'),
            st('cp'), st('cs'), st('srp'), st('sip'), st('srs'), st('sis'))
```
